```python
import jax, jax.numpy as jnp
from jax import lax
import numpy as np

D_MODEL = 2048
BATCH = 8
SEQ = 4096
DEPTH = 4

CHUNK = 64
N_MIXERS = 2
N_A = (DEPTH + N_MIXERS - 1) // N_MIXERS
N_B = DEPTH // N_MIXERS
NORM_EPS = 1e-6

SGU_BLOCK = 128
SGU_WIDTH = 2 * D_MODEL
SGU_GROUPS = 16
SGU_GROUP_DIM = SGU_WIDTH // SGU_GROUPS

MLA_HEADS = 16
Q_LORA_RANK = 448
KV_LORA_RANK = 512
QK_NOPE_DIM = 128
QK_ROPE_DIM = 64
V_HEAD_DIM = 128
MLA_WIDTH = MLA_HEADS * V_HEAD_DIM
ROPE_THETA = 10000.0
Q_BLOCK = 128

kernel_name = "hybrid_sgu_mla_adaln_sandwich"


def rms_norm(x, g):
    xf = x.astype(jnp.float32)
    y = xf * lax.rsqrt(jnp.mean(xf * xf, axis=-1, keepdims=True) + NORM_EPS)
    return (y * g.astype(jnp.float32)).astype(x.dtype)


def layer_norm(x, g):
    xf = x.astype(jnp.float32)
    mu = jnp.mean(xf, axis=-1, keepdims=True)
    var = jnp.mean(jnp.square(xf - mu), axis=-1, keepdims=True)
    return ((xf - mu) * lax.rsqrt(var + NORM_EPS) * g.astype(jnp.float32)).astype(x.dtype)


def apply_rope(x, cos, sin):
    xf = x.astype(jnp.float32)
    x1, x2 = jnp.split(xf, 2, axis=-1)
    return jnp.concatenate([x1 * cos - x2 * sin, x1 * sin + x2 * cos], axis=-1).astype(x.dtype)


def sgu_mixer(h, w_in, norm_g, w_s, b_s, w_out):
    B, S, _ = h.shape
    u, v, z = jnp.split(h @ w_in, 3, axis=-1)
    u = jax.nn.gelu(u, approximate=False)
    v = layer_norm(jax.nn.gelu(v, approximate=False), norm_g)
    t_chunk = jnp.arange(SGU_BLOCK) // CHUNK
    mask = (t_chunk[None, :] <= t_chunk[:, None]).astype(w_s.dtype)
    w = w_s * mask[None]
    vb = v.reshape(B, S // SGU_BLOCK, SGU_BLOCK, SGU_GROUPS, SGU_GROUP_DIM)
    vm = jnp.einsum('gts,bnsgc->bntgc', w, vb) + b_s.T[:, :, None]
    y = u * vm.reshape(B, S, SGU_WIDTH) * jax.nn.silu(z)
    return y @ w_out


def chunk_causal_attention(q_nope, q_rope, k_nope, k_rope, v):
    B, S, H, _ = q_nope.shape
    nqb = S // Q_BLOCK
    scale = (QK_NOPE_DIM + QK_ROPE_DIM) ** -0.5
    k_chunk = jnp.arange(S) // CHUNK

    def to_blocks(t):
        return jnp.moveaxis(t.reshape(B, nqb, Q_BLOCK, *t.shape[2:]), 1, 0)

    def one_block(args):
        idx, qn, qr = args
        s = (jnp.einsum('bqhd,bkhd->bhqk', qn, k_nope)
             + jnp.einsum('bqhd,bkd->bhqk', qr, k_rope)).astype(jnp.float32) * scale
        q_chunk = (idx * Q_BLOCK + jnp.arange(Q_BLOCK)) // CHUNK
        mask = k_chunk[None, :] <= q_chunk[:, None]
        p = jax.nn.softmax(jnp.where(mask, s, -1e30), axis=-1).astype(v.dtype)
        return jnp.einsum('bhqk,bkhd->bqhd', p, v)

    out = lax.map(one_block, (jnp.arange(nqb), to_blocks(q_nope), to_blocks(q_rope)))
    return jnp.moveaxis(out, 0, 1).reshape(B, S, H, V_HEAD_DIM)


def mla_mixer(h, w_in, q_norm_g, kv_norm_g, w_uq, w_ukv, w_out):
    B, S, _ = h.shape
    o1 = Q_LORA_RANK
    o2 = o1 + KV_LORA_RANK
    o3 = o2 + QK_ROPE_DIM
    cq, ckv, k_rope, z = jnp.split(h @ w_in, [o1, o2, o3], axis=-1)
    q = (rms_norm(cq, q_norm_g) @ w_uq).reshape(B, S, MLA_HEADS, QK_NOPE_DIM + QK_ROPE_DIM)
    kv = (rms_norm(ckv, kv_norm_g) @ w_ukv).reshape(B, S, MLA_HEADS, QK_NOPE_DIM + V_HEAD_DIM)
    q_nope, q_rope = q[..., :QK_NOPE_DIM], q[..., QK_NOPE_DIM:]
    k_nope, v = kv[..., :QK_NOPE_DIM], kv[..., QK_NOPE_DIM:]
    pos = jnp.arange(S, dtype=jnp.float32)
    inv_freq = ROPE_THETA ** (-jnp.arange(0, QK_ROPE_DIM, 2, dtype=jnp.float32) / QK_ROPE_DIM)
    ang = pos[:, None] * inv_freq[None, :]
    cos, sin = jnp.cos(ang), jnp.sin(ang)
    q_rope = apply_rope(q_rope, cos[:, None, :], sin[:, None, :])
    k_rope = apply_rope(k_rope, cos, sin)
    o = chunk_causal_attention(q_nope, q_rope, k_nope, k_rope, v)
    y = o.reshape(B, S, MLA_WIDTH) * jax.nn.silu(z)
    return y @ w_out


def setup_inputs(seed: int = 0) -> dict:
    key = jax.random.key(seed)
    ks = jax.random.split(key, 17)
    f32 = jnp.float32
    D = D_MODEL

    def nrm(k, shape, s):
        return jax.random.normal(k, shape, f32) * s

    def gain(k, shape):
        return 1.0 + 0.05 * jax.random.normal(k, shape, f32)

    mla_in_cols = Q_LORA_RANK + KV_LORA_RANK + QK_ROPE_DIM + MLA_WIDTH
    return {
        'x': nrm(ks[0], (BATCH, SEQ, D), 1.0),
        'c': nrm(ks[1], (BATCH, D), 1.0),
        'ada_w': nrm(ks[2], (DEPTH, D, 3 * D), 0.5 * D ** -0.5),
        'ada_b': nrm(ks[3], (DEPTH, 3 * D), 0.01),
        'pre_g': gain(ks[4], (DEPTH, D)),
        'post_g': gain(ks[5], (DEPTH, D)),
        'sgu_w_in': nrm(ks[6], (N_A, D, 3 * SGU_WIDTH), D ** -0.5),
        'sgu_norm_g': gain(ks[7], (N_A, SGU_WIDTH)),
        'sgu_w_s': nrm(ks[8], (N_A, SGU_GROUPS, SGU_BLOCK, SGU_BLOCK), SGU_BLOCK ** -0.5),
        'sgu_b_s': gain(ks[9], (N_A, SGU_GROUPS, SGU_BLOCK)),
        'sgu_w_out': nrm(ks[10], (N_A, SGU_WIDTH, D), SGU_WIDTH ** -0.5),
        'mla_w_in': nrm(ks[11], (N_B, D, mla_in_cols), D ** -0.5),
        'mla_q_norm_g': gain(ks[12], (N_B, Q_LORA_RANK)),
        'mla_kv_norm_g': gain(ks[13], (N_B, KV_LORA_RANK)),
        'mla_w_uq': nrm(ks[14], (N_B, Q_LORA_RANK, MLA_HEADS * (QK_NOPE_DIM + QK_ROPE_DIM)), Q_LORA_RANK ** -0.5),
        'mla_w_ukv': nrm(ks[15], (N_B, KV_LORA_RANK, MLA_HEADS * (QK_NOPE_DIM + V_HEAD_DIM)), KV_LORA_RANK ** -0.5),
        'mla_w_out': nrm(ks[16], (N_B, MLA_WIDTH, D), MLA_WIDTH ** -0.5),
    }


def reference(x, c, ada_w, ada_b, pre_g, post_g, sgu_w_in, sgu_norm_g, sgu_w_s, sgu_b_s, sgu_w_out,
              mla_w_in, mla_q_norm_g, mla_kv_norm_g, mla_w_uq, mla_w_ukv, mla_w_out):
    cond = jax.nn.silu(c)
    for i in range(DEPTH):
        mod = cond @ ada_w[i] + ada_b[i]
        shift, scale, gate = jnp.split(mod, 3, axis=-1)
        h = rms_norm(x, pre_g[i]) * (1 + scale[:, None, :]) + shift[:, None, :]
        j = i // N_MIXERS
        if i % N_MIXERS == 0:
            y = sgu_mixer(h, sgu_w_in[j], sgu_norm_g[j], sgu_w_s[j], sgu_b_s[j], sgu_w_out[j])
        else:
            y = mla_mixer(h, mla_w_in[j], mla_q_norm_g[j], mla_kv_norm_g[j], mla_w_uq[j], mla_w_ukv[j], mla_w_out[j])
        x = x + gate[:, None, :] * rms_norm(y, post_g[i])
    return x
```

```python
import functools

import jax
import jax.numpy as jnp
from jax import lax
from jax.experimental import pallas as pl
from jax.experimental.pallas import tpu as pltpu

F32 = jnp.float32
BF16 = jnp.bfloat16

NORM_EPS = 1e-6
CHUNK = 64
SGU_BLOCK = 128
SGU_GROUP_DIM = 256
Q_LORA_RANK = 448
KV_LORA_RANK = 512
QK_NOPE_DIM = 128
QK_ROPE_DIM = 64
V_HEAD_DIM = 128
ROPE_THETA = 10000.0
LANES = 128
LATENT_PAD = 512
HEAD_PAD = 2 * LANES
ROPE_LANE0 = LANES - QK_ROPE_DIM
VMEM_LIMIT_BYTES = 56 * 1024 * 1024


def _tile(n, pref, mult=8):
    t = min(n, pref)
    while t > mult and (n % t or t % mult):
        t -= mult
    return t if n % t == 0 else n


def _params(*semantics):
    return pltpu.CompilerParams(dimension_semantics=semantics, vmem_limit_bytes=VMEM_LIMIT_BYTES)


def _gelu(x):
    return 0.5 * x * (1.0 + lax.erf(x * (0.5 ** 0.5)))


def _silu(x):
    return x * jax.nn.sigmoid(x)


def _modulated_norm(x, g, shift, scale):
    ms = jnp.mean(x * x, axis=-1, keepdims=True)
    y = x * lax.rsqrt(ms + NORM_EPS) * g
    return y * (1.0 + scale) + shift


def _rope_rotate(x, cos_t, sin_t):
    lane = lax.broadcasted_iota(jnp.int32, x.shape, 1)
    half = QK_ROPE_DIM // 2
    partner = jnp.where(lane < ROPE_LANE0 + half,
                        pltpu.roll(x, LANES - half, 1),
                        pltpu.roll(x, half, 1))
    return x * cos_t + partner * sin_t


def _mod_kernel(c_ref, w_ref, b_ref, o_ref):
    cond = _silu(c_ref[...]).astype(BF16)
    o_ref[0] = jnp.dot(cond, w_ref[0].astype(BF16), preferred_element_type=F32) + b_ref[0]


def _adaln_mod(c, ada_w, ada_b):
    depth, d, n = ada_w.shape
    b = c.shape[0]
    tn = _tile(n, 1536, LANES)
    return pl.pallas_call(
        _mod_kernel,
        grid=(depth, n // tn),
        in_specs=[
            pl.BlockSpec((b, d), lambda i, j: (0, 0)),
            pl.BlockSpec((1, d, tn), lambda i, j: (i, 0, j)),
            pl.BlockSpec((1, 1, tn), lambda i, j: (i, 0, j)),
        ],
        out_specs=pl.BlockSpec((1, b, tn), lambda i, j: (i, 0, j)),
        out_shape=jax.ShapeDtypeStruct((depth, b, n), F32),
        compiler_params=_params("arbitrary", "arbitrary"),
        name="adaln_mod",
    )(c, ada_w, ada_b.reshape(depth, 1, n))


def _mod_specs(tiles_per_batch, d):
    def spec(k):
        return pl.BlockSpec((None, None, 1, d), lambda m, *_: (m // tiles_per_batch, k, 0, 0))
    return spec(0), spec(1), spec(2)


def _sgu_v_kernel(x_ref, g_ref, shift_ref, scale_ref, w_ref, h_ref, vg_ref, mu_ref, rstd_ref,
                  s1_ref, s2_ref, *, width):
    n = pl.program_id(1)

    @pl.when(n == 0)
    def _():
        h = _modulated_norm(x_ref[...], g_ref[...], shift_ref[...], scale_ref[...])
        h_ref[...] = h.astype(BF16)
        s1_ref[...] = jnp.zeros_like(s1_ref)
        s2_ref[...] = jnp.zeros_like(s2_ref)

    v = _gelu(jnp.dot(h_ref[...], w_ref[...], preferred_element_type=F32))
    vg_ref[...] = v.astype(BF16)
    s1_ref[...] += jnp.sum(v, axis=-1, keepdims=True)
    s2_ref[...] += jnp.sum(v * v, axis=-1, keepdims=True)

    @pl.when(n == pl.num_programs(1) - 1)
    def _():
        mu = s1_ref[...] * (1.0 / width)
        var = s2_ref[...] * (1.0 / width) - mu * mu
        mu_ref[...] = mu
        rstd_ref[...] = lax.rsqrt(var + NORM_EPS)


def _sgu_v(x, pre_g, mod, w_in, seq):
    m, d = x.shape
    e = w_in.shape[1] // 3
    tm = _tile(seq, 1024, SGU_BLOCK)
    tn = _tile(e, 1024, LANES)
    shift_spec, scale_spec, _ = _mod_specs(seq // tm, d)
    return pl.pallas_call(
        functools.partial(_sgu_v_kernel, width=e),
        grid=(m // tm, e // tn),
        in_specs=[
            pl.BlockSpec((tm, d), lambda i, j: (i, 0)),
            pl.BlockSpec((1, d), lambda i, j: (0, 0)),
            shift_spec, scale_spec,
            pl.BlockSpec((d, tn), lambda i, j: (0, e // tn + j)),
        ],
        out_specs=[
            pl.BlockSpec((tm, d), lambda i, j: (i, 0)),
            pl.BlockSpec((tm, tn), lambda i, j: (i, j)),
            pl.BlockSpec((tm, 1), lambda i, j: (i, 0)),
            pl.BlockSpec((tm, 1), lambda i, j: (i, 0)),
        ],
        out_shape=[
            jax.ShapeDtypeStruct((m, d), BF16),
            jax.ShapeDtypeStruct((m, e), BF16),
            jax.ShapeDtypeStruct((m, 1), F32),
            jax.ShapeDtypeStruct((m, 1), F32),
        ],
        scratch_shapes=[pltpu.VMEM((tm, 1), F32), pltpu.VMEM((tm, 1), F32)],
        compiler_params=_params("arbitrary", "arbitrary"),
        name="sgu_v",
    )(x, pre_g, mod, mod, w_in)


def _sgu_gate_kernel(h_ref, wu_ref, wz_ref, vg_ref, mu_ref, rstd_ref, ng_ref, ws_ref, bs_ref,
                     y_ref, *, groups_per_step):
    h = h_ref[...]
    tm = h.shape[0]
    nblk = tm // SGU_BLOCK
    u = _gelu(jnp.dot(h, wu_ref[...], preferred_element_type=F32))
    z = _silu(jnp.dot(h, wz_ref[...], preferred_element_type=F32))
    vn = ((vg_ref[...].astype(F32) - mu_ref[...]) * rstd_ref[...] * ng_ref[...]).astype(BF16)

    t_chunk = lax.broadcasted_iota(jnp.int32, (SGU_BLOCK, SGU_BLOCK), 0) // CHUNK
    s_chunk = lax.broadcasted_iota(jnp.int32, (SGU_BLOCK, SGU_BLOCK), 1) // CHUNK
    causal = s_chunk <= t_chunk
    for gi in range(groups_per_step):
        cols = slice(gi * SGU_GROUP_DIM, (gi + 1) * SGU_GROUP_DIM)
        w = jnp.where(causal, ws_ref[gi], 0.0).astype(BF16)
        vcat = jnp.concatenate(
            [vn[b * SGU_BLOCK:(b + 1) * SGU_BLOCK, cols] for b in range(nblk)], axis=1)
        vm = jnp.dot(w, vcat, preferred_element_type=F32) + bs_ref[gi]
        for b in range(nblk):
            rows = slice(b * SGU_BLOCK, (b + 1) * SGU_BLOCK)
            vmb = vm[:, b * SGU_GROUP_DIM:(b + 1) * SGU_GROUP_DIM]
            y_ref[rows, cols] = (u[rows, cols] * vmb * z[rows, cols]).astype(BF16)


def _sgu_gate(h, w_in, vg, mu, rstd, norm_g, w_s, b_s, seq):
    m, d = h.shape
    e = w_in.shape[1] // 3
    groups = e // SGU_GROUP_DIM
    gs = 2 if groups % 2 == 0 else 1
    tn = gs * SGU_GROUP_DIM
    tm = _tile(seq, 1024, SGU_BLOCK)
    return pl.pallas_call(
        functools.partial(_sgu_gate_kernel, groups_per_step=gs),
        grid=(m // tm, groups // gs),
        in_specs=[
            pl.BlockSpec((tm, d), lambda i, j: (i, 0)),
            pl.BlockSpec((d, tn), lambda i, j: (0, j)),
            pl.BlockSpec((d, tn), lambda i, j: (0, 2 * (e // tn) + j)),
            pl.BlockSpec((tm, tn), lambda i, j: (i, j)),
            pl.BlockSpec((tm, 1), lambda i, j: (i, 0)),
            pl.BlockSpec((tm, 1), lambda i, j: (i, 0)),
            pl.BlockSpec((1, tn), lambda i, j: (0, j)),
            pl.BlockSpec((gs, SGU_BLOCK, SGU_BLOCK), lambda i, j: (j, 0, 0)),
            pl.BlockSpec((gs, SGU_BLOCK, 1), lambda i, j: (j, 0, 0)),
        ],
        out_specs=pl.BlockSpec((tm, tn), lambda i, j: (i, j)),
        out_shape=jax.ShapeDtypeStruct((m, e), BF16),
        compiler_params=_params("arbitrary", "arbitrary"),
        name="sgu_gate",
    )(h, w_in, w_in, vg, mu, rstd, norm_g, w_s, b_s.reshape(groups, SGU_BLOCK, 1))


def _out_proj_kernel(y_ref, w_ref, x_ref, g_ref, gate_ref, o_ref):
    acc = jnp.dot(y_ref[...], w_ref[...], preferred_element_type=F32)
    ms = jnp.mean(acc * acc, axis=-1, keepdims=True)
    yn = acc * lax.rsqrt(ms + NORM_EPS) * g_ref[...]
    o_ref[...] = x_ref[...] + gate_ref[...] * yn


def _out_proj(y, w_out, x, post_g, mod, seq):
    m, k = y.shape
    d = w_out.shape[1]
    tm = _tile(seq, 512)
    _, _, gate_spec = _mod_specs(seq // tm, d)
    return pl.pallas_call(
        _out_proj_kernel,
        grid=(m // tm,),
        in_specs=[
            pl.BlockSpec((tm, k), lambda i: (i, 0)),
            pl.BlockSpec((k, d), lambda i: (0, 0), pipeline_mode=pl.Buffered(1)),
            pl.BlockSpec((tm, d), lambda i: (i, 0)),
            pl.BlockSpec((1, d), lambda i: (0, 0)),
            gate_spec,
        ],
        out_specs=pl.BlockSpec((tm, d), lambda i: (i, 0)),
        out_shape=jax.ShapeDtypeStruct((m, d), F32),
        compiler_params=_params("arbitrary"),
        name="out_proj",
    )(y, w_out, x, post_g, mod)


def _mla_in_kernel(x_ref, g_ref, shift_ref, scale_ref, w_ref, gq_ref, gkv_ref, cos_ref, sin_ref,
                   cq_ref, ckv_ref, kr_ref, sz_ref):
    h = _modulated_norm(x_ref[...], g_ref[...], shift_ref[...], scale_ref[...]).astype(BF16)
    nlat = 2 * LATENT_PAD + LANES
    lat = jnp.dot(h, w_ref[:, :nlat], preferred_element_type=F32)
    cq = lat[:, :LATENT_PAD]
    ckv = lat[:, LATENT_PAD:2 * LATENT_PAD]
    kr = lat[:, 2 * LATENT_PAD:]
    ms_q = jnp.sum(cq * cq, axis=-1, keepdims=True) * (1.0 / Q_LORA_RANK)
    cq_ref[...] = (cq * lax.rsqrt(ms_q + NORM_EPS) * gq_ref[...]).astype(BF16)
    ms_kv = jnp.mean(ckv * ckv, axis=-1, keepdims=True)
    ckv_ref[...] = (ckv * lax.rsqrt(ms_kv + NORM_EPS) * gkv_ref[...]).astype(BF16)
    kr_ref[...] = _rope_rotate(kr, cos_ref[...], sin_ref[...]).astype(BF16)
    z = jnp.dot(h, w_ref[:, nlat:], preferred_element_type=F32)
    sz_ref[...] = _silu(z).astype(BF16)


def _mla_in(x, pre_g, mod, w_in, gq, gkv, cos_t, sin_t, seq):
    m, d = x.shape
    ncols = w_in.shape[1]
    width = ncols - 2 * LATENT_PAD - LANES
    tm = _tile(seq, 512)
    tpb = seq // tm
    shift_spec, scale_spec, _ = _mod_specs(tpb, d)
    row = lambda n: pl.BlockSpec((tm, n), lambda i: (i, 0))
    const = lambda r, n: pl.BlockSpec((r, n), lambda i: (0, 0))
    table = pl.BlockSpec((tm, LANES), lambda i: (i % tpb, 0))
    return pl.pallas_call(
        _mla_in_kernel,
        grid=(m // tm,),
        in_specs=[
            row(d), const(1, d), shift_spec, scale_spec,
            pl.BlockSpec((d, ncols), lambda i: (0, 0), pipeline_mode=pl.Buffered(1)),
            const(1, LATENT_PAD), const(1, LATENT_PAD), table, table,
        ],
        out_specs=[row(LATENT_PAD), row(LATENT_PAD), row(LANES), row(width)],
        out_shape=[
            jax.ShapeDtypeStruct((m, LATENT_PAD), BF16),
            jax.ShapeDtypeStruct((m, LATENT_PAD), BF16),
            jax.ShapeDtypeStruct((m, LANES), BF16),
            jax.ShapeDtypeStruct((m, width), BF16),
        ],
        compiler_params=_params("arbitrary"),
        name="mla_in",
    )(x, pre_g, mod, mod, w_in, gq, gkv, cos_t, sin_t)


def _mla_up_kernel(cq_ref, ckv_ref, kr_ref, wq_ref, wkv_ref, cos_ref, sin_ref,
                   q_ref, k_ref, v_ref, *, heads, scale):
    q = jnp.dot(cq_ref[...], wq_ref[...], preferred_element_type=F32) * scale
    kv = jnp.dot(ckv_ref[...], wkv_ref[...], preferred_element_type=F32)
    cos_t = cos_ref[...]
    sin_t = sin_ref[...]
    kr = kr_ref[...]
    for hd in range(heads):
        c0 = hd * HEAD_PAD
        q_ref[:, c0:c0 + LANES] = q[:, c0:c0 + LANES].astype(BF16)
        q_ref[:, c0 + LANES:c0 + HEAD_PAD] = _rope_rotate(
            q[:, c0 + LANES:c0 + HEAD_PAD], cos_t, sin_t).astype(BF16)
        k_ref[:, c0:c0 + LANES] = kv[:, hd * QK_NOPE_DIM:(hd + 1) * QK_NOPE_DIM].astype(BF16)
        k_ref[:, c0 + LANES:c0 + HEAD_PAD] = kr
    v_ref[...] = kv[:, heads * QK_NOPE_DIM:].astype(BF16)


def _mla_up(cq, ckv, kr, wq, wkv, cos_t, sin_t, seq, heads):
    m = cq.shape[0]
    tm = _tile(seq, 512)
    tpb = seq // tm
    row = lambda n: pl.BlockSpec((tm, n), lambda i: (i, 0))
    whole = lambda a: pl.BlockSpec(a.shape, lambda i: (0, 0), pipeline_mode=pl.Buffered(1))
    table = pl.BlockSpec((tm, LANES), lambda i: (i % tpb, 0))
    scale = float((QK_NOPE_DIM + QK_ROPE_DIM) ** -0.5)
    return pl.pallas_call(
        functools.partial(_mla_up_kernel, heads=heads, scale=scale),
        grid=(m // tm,),
        in_specs=[row(LATENT_PAD), row(LATENT_PAD), row(LANES), whole(wq), whole(wkv), table, table],
        out_specs=[row(heads * HEAD_PAD), row(heads * HEAD_PAD), row(heads * V_HEAD_DIM)],
        out_shape=[
            jax.ShapeDtypeStruct((m, heads * HEAD_PAD), BF16),
            jax.ShapeDtypeStruct((m, heads * HEAD_PAD), BF16),
            jax.ShapeDtypeStruct((m, heads * V_HEAD_DIM), BF16),
        ],
        compiler_params=_params("arbitrary"),
        name="mla_up",
    )(cq, ckv, kr, wq, wkv, cos_t, sin_t)


def _attn_kernel(q_ref, k_ref, v_ref, sz_ref, o_ref, m_ref, l_ref, acc_ref, *, tk):
    i = pl.program_id(2)
    tq = q_ref.shape[0]
    nsub = tq // tk
    q = q_ref[...]
    m_ref[...] = jnp.full_like(m_ref, -jnp.inf)
    l_ref[...] = jnp.zeros_like(l_ref)
    acc_ref[...] = jnp.zeros_like(acc_ref)

    def step(j, masked):
        k = k_ref[pl.ds(pl.multiple_of(j * tk, tk), tk), :]
        v = v_ref[pl.ds(pl.multiple_of(j * tk, tk), tk), :]
        s = lax.dot_general(q, k, (((1,), (1,)), ((), ())), preferred_element_type=F32)
        if masked:
            q_chunk = (i * tq + lax.broadcasted_iota(jnp.int32, s.shape, 0)) // CHUNK
            k_chunk = (j * tk + lax.broadcasted_iota(jnp.int32, s.shape, 1)) // CHUNK
            s = jnp.where(k_chunk <= q_chunk, s, -1e30)
        m_prev = m_ref[...]
        m_new = jnp.maximum(m_prev, jnp.max(s, axis=-1, keepdims=True))
        alpha = jnp.exp(m_prev - m_new)
        p = jnp.exp(s - m_new)
        l_ref[...] = alpha * l_ref[...] + jnp.sum(p, axis=-1, keepdims=True)
        acc_ref[...] = alpha * acc_ref[...] + jnp.dot(p.astype(BF16), v, preferred_element_type=F32)
        m_ref[...] = m_new

    def body(j, carry):
        step(j, False)
        return carry

    lax.fori_loop(0, i * nsub, body, 0)
    for dsub in range(nsub):
        step(i * nsub + dsub, True)
    o_ref[...] = (acc_ref[...] / l_ref[...] * sz_ref[...].astype(F32)).astype(BF16)


def _attention(q, k, v, sz, batch, seq, heads):
    m = q.shape[0]
    tq = _tile(seq, 512, CHUNK)
    tk = _tile(tq, 512, CHUNK)
    nq = seq // tq
    return pl.pallas_call(
        functools.partial(_attn_kernel, tk=tk),
        grid=(batch, heads, nq),
        in_specs=[
            pl.BlockSpec((tq, HEAD_PAD), lambda b, h, i: (b * nq + i, h)),
            pl.BlockSpec((seq, HEAD_PAD), lambda b, h, i: (b, h)),
            pl.BlockSpec((seq, V_HEAD_DIM), lambda b, h, i: (b, h)),
            pl.BlockSpec((tq, V_HEAD_DIM), lambda b, h, i: (b * nq + i, h)),
        ],
        out_specs=pl.BlockSpec((tq, V_HEAD_DIM), lambda b, h, i: (b * nq + i, h)),
        out_shape=jax.ShapeDtypeStruct((m, heads * V_HEAD_DIM), BF16),
        scratch_shapes=[
            pltpu.VMEM((tq, 1), F32),
            pltpu.VMEM((tq, 1), F32),
            pltpu.VMEM((tq, V_HEAD_DIM), F32),
        ],
        compiler_params=_params("arbitrary", "arbitrary", "arbitrary"),
        name="mla_attention",
    )(q, k, v, sz)


def _rope_tables(seq):
    pos = jnp.arange(seq, dtype=F32)
    inv_freq = ROPE_THETA ** (-jnp.arange(0, QK_ROPE_DIM, 2, dtype=F32) / QK_ROPE_DIM)
    ang = pos[:, None] * inv_freq[None, :]
    cos, sin = jnp.cos(ang), jnp.sin(ang)
    ones = jnp.ones((seq, ROPE_LANE0), F32)
    zeros = jnp.zeros((seq, ROPE_LANE0), F32)
    return (jnp.concatenate([ones, cos, cos], axis=-1),
            jnp.concatenate([zeros, -sin, sin], axis=-1))


def _mla_weight_layout(w_in, gq, w_uq, w_ukv, heads):
    nb, d, _ = w_in.shape
    o1, o2, o3 = Q_LORA_RANK, Q_LORA_RANK + KV_LORA_RANK, Q_LORA_RANK + KV_LORA_RANK + QK_ROPE_DIM
    zc = lambda n: jnp.zeros((nb, d, n), w_in.dtype)
    w_in_l = jnp.concatenate([
        w_in[..., :o1], zc(LATENT_PAD - Q_LORA_RANK),
        w_in[..., o1:o2], zc(LATENT_PAD - KV_LORA_RANK),
        zc(ROPE_LANE0), w_in[..., o2:o3],
        w_in[..., o3:],
    ], axis=-1).astype(BF16)
    gq_l = jnp.pad(gq, ((0, 0), (0, LATENT_PAD - Q_LORA_RANK)))
    wq = w_uq.reshape(nb, Q_LORA_RANK, heads, QK_NOPE_DIM + QK_ROPE_DIM)
    wq = jnp.concatenate([
        wq[..., :QK_NOPE_DIM],
        jnp.zeros((nb, Q_LORA_RANK, heads, ROPE_LANE0), wq.dtype),
        wq[..., QK_NOPE_DIM:],
    ], axis=-1).reshape(nb, Q_LORA_RANK, heads * HEAD_PAD)
    wq_l = jnp.pad(wq, ((0, 0), (0, LATENT_PAD - Q_LORA_RANK), (0, 0))).astype(BF16)
    wkv = w_ukv.reshape(nb, KV_LORA_RANK, heads, QK_NOPE_DIM + V_HEAD_DIM)
    wkv_l = jnp.concatenate([
        wkv[..., :QK_NOPE_DIM].reshape(nb, KV_LORA_RANK, heads * QK_NOPE_DIM),
        wkv[..., QK_NOPE_DIM:].reshape(nb, KV_LORA_RANK, heads * V_HEAD_DIM),
    ], axis=-1).astype(BF16)
    return w_in_l, gq_l, wq_l, wkv_l


def kernel(x, c, ada_w, ada_b, pre_g, post_g, sgu_w_in, sgu_norm_g, sgu_w_s, sgu_b_s, sgu_w_out,
           mla_w_in, mla_q_norm_g, mla_kv_norm_g, mla_w_uq, mla_w_ukv, mla_w_out):
    batch, seq, d = x.shape
    depth = ada_w.shape[0]
    heads = mla_w_out.shape[1] // V_HEAD_DIM
    assert seq % SGU_BLOCK == 0 and d % LANES == 0
    assert sgu_w_in.shape[2] % (3 * SGU_GROUP_DIM) == 0
    assert mla_kv_norm_g.shape[1] == KV_LORA_RANK == LATENT_PAD

    mod = _adaln_mod(c, ada_w, ada_b).reshape(depth, batch, 3, 1, d)
    sgu_w_in_l = sgu_w_in.astype(BF16)
    sgu_w_out_l = sgu_w_out.astype(BF16)
    mla_w_in_l, gq_l, wq_l, wkv_l = _mla_weight_layout(mla_w_in, mla_q_norm_g, mla_w_uq, mla_w_ukv, heads)
    mla_w_out_l = mla_w_out.astype(BF16)
    cos_t, sin_t = _rope_tables(seq)

    xf = x.reshape(batch * seq, d)
    for i in range(depth):
        j = i // 2
        pg = pre_g[i][None, :]
        if i % 2 == 0:
            h, vg, mu, rstd = _sgu_v(xf, pg, mod[i], sgu_w_in_l[j], seq)
            y = _sgu_gate(h, sgu_w_in_l[j], vg, mu, rstd, sgu_norm_g[j][None, :], sgu_w_s[j],
                          sgu_b_s[j], seq)
            w_out = sgu_w_out_l[j]
        else:
            cq, ckv, kr, sz = _mla_in(xf, pg, mod[i], mla_w_in_l[j], gq_l[j][None, :],
                                      mla_kv_norm_g[j][None, :], cos_t, sin_t, seq)
            q, k, v = _mla_up(cq, ckv, kr, wq_l[j], wkv_l[j], cos_t, sin_t, seq, heads)
            y = _attention(q, k, v, sz, batch, seq, heads)
            w_out = mla_w_out_l[j]
        xf = _out_proj(y, w_out, xf, post_g[i][None, :], mod[i], seq)
    return xf.reshape(batch, seq, d)
```

```python
import functools

import jax
import jax.numpy as jnp
from jax import lax
from jax.experimental import pallas as pl
from jax.experimental.pallas import tpu as pltpu

F32 = jnp.float32
BF16 = jnp.bfloat16

NORM_EPS = 1e-6
CHUNK = 64
SGU_BLOCK = 128
SGU_GROUP_DIM = 256
Q_LORA_RANK = 448
KV_LORA_RANK = 512
QK_NOPE_DIM = 128
QK_ROPE_DIM = 64
V_HEAD_DIM = 128
ROPE_THETA = 10000.0
LANES = 128
LATENT_PAD = 512
HEAD_PAD = 2 * LANES
ROPE_LANE0 = LANES - QK_ROPE_DIM
VMEM_LIMIT_BYTES = 56 * 1024 * 1024


def _tile(n, pref, mult=8):
    t = min(n, pref)
    while t > mult and (n % t or t % mult):
        t -= mult
    return t if n % t == 0 else n


def _params(*semantics):
    return pltpu.CompilerParams(dimension_semantics=semantics, vmem_limit_bytes=VMEM_LIMIT_BYTES)


def _gelu(x):
    return 0.5 * x * (1.0 + lax.erf(x * (0.5 ** 0.5)))


def _silu(x):
    return x * jax.nn.sigmoid(x)


def _modulated_norm(x, g, shift, scale):
    ms = jnp.mean(x * x, axis=-1, keepdims=True)
    y = x * lax.rsqrt(ms + NORM_EPS) * g
    return y * (1.0 + scale) + shift


def _rope_rotate(x, cos_t, sin_t):
    lane = lax.broadcasted_iota(jnp.int32, x.shape, 1)
    half = QK_ROPE_DIM // 2
    partner = jnp.where(lane < ROPE_LANE0 + half,
                        pltpu.roll(x, LANES - half, 1),
                        pltpu.roll(x, half, 1))
    return x * cos_t + partner * sin_t


def _mod_kernel(c_ref, w_ref, b_ref, o_ref):
    cond = _silu(c_ref[...]).astype(BF16)
    o_ref[0] = jnp.dot(cond, w_ref[0].astype(BF16), preferred_element_type=F32) + b_ref[0]


def _adaln_mod(c, ada_w, ada_b):
    depth, d, n = ada_w.shape
    b = c.shape[0]
    tn = _tile(n, 1536, LANES)
    return pl.pallas_call(
        _mod_kernel,
        grid=(depth, n // tn),
        in_specs=[
            pl.BlockSpec((b, d), lambda i, j: (0, 0)),
            pl.BlockSpec((1, d, tn), lambda i, j: (i, 0, j)),
            pl.BlockSpec((1, 1, tn), lambda i, j: (i, 0, j)),
        ],
        out_specs=pl.BlockSpec((1, b, tn), lambda i, j: (i, 0, j)),
        out_shape=jax.ShapeDtypeStruct((depth, b, n), F32),
        compiler_params=_params("arbitrary", "arbitrary"),
        name="adaln_mod",
    )(c, ada_w, ada_b.reshape(depth, 1, n))


def _mod_specs(tiles_per_batch, d):
    def spec(k):
        return pl.BlockSpec((None, None, 1, d), lambda m, *_: (m // tiles_per_batch, k, 0, 0))
    return spec(0), spec(1), spec(2)


def _sgu_v_kernel(x_ref, g_ref, shift_ref, scale_ref, w_ref, h_ref, vg_ref, mu_ref, rstd_ref,
                  s1_ref, s2_ref, *, width):
    n = pl.program_id(1)

    @pl.when(n == 0)
    def _():
        h = _modulated_norm(x_ref[...], g_ref[...], shift_ref[...], scale_ref[...])
        h_ref[...] = h.astype(BF16)
        s1_ref[...] = jnp.zeros_like(s1_ref)
        s2_ref[...] = jnp.zeros_like(s2_ref)

    v = _gelu(jnp.dot(h_ref[...], w_ref[...], preferred_element_type=F32))
    vg_ref[...] = v.astype(BF16)
    s1_ref[...] += jnp.sum(v, axis=-1, keepdims=True)
    s2_ref[...] += jnp.sum(v * v, axis=-1, keepdims=True)

    @pl.when(n == pl.num_programs(1) - 1)
    def _():
        mu = s1_ref[...] * (1.0 / width)
        var = s2_ref[...] * (1.0 / width) - mu * mu
        mu_ref[...] = mu
        rstd_ref[...] = lax.rsqrt(var + NORM_EPS)


def _sgu_v(x, pre_g, mod, w_in, seq):
    m, d = x.shape
    e = w_in.shape[1] // 3
    tm = _tile(seq, 1024, SGU_BLOCK)
    tn = _tile(e, 1024, LANES)
    shift_spec, scale_spec, _ = _mod_specs(seq // tm, d)
    return pl.pallas_call(
        functools.partial(_sgu_v_kernel, width=e),
        grid=(m // tm, e // tn),
        in_specs=[
            pl.BlockSpec((tm, d), lambda i, j: (i, 0)),
            pl.BlockSpec((1, d), lambda i, j: (0, 0)),
            shift_spec, scale_spec,
            pl.BlockSpec((d, tn), lambda i, j: (0, e // tn + j)),
        ],
        out_specs=[
            pl.BlockSpec((tm, d), lambda i, j: (i, 0)),
            pl.BlockSpec((tm, tn), lambda i, j: (i, j)),
            pl.BlockSpec((tm, 1), lambda i, j: (i, 0)),
            pl.BlockSpec((tm, 1), lambda i, j: (i, 0)),
        ],
        out_shape=[
            jax.ShapeDtypeStruct((m, d), BF16),
            jax.ShapeDtypeStruct((m, e), BF16),
            jax.ShapeDtypeStruct((m, 1), F32),
            jax.ShapeDtypeStruct((m, 1), F32),
        ],
        scratch_shapes=[pltpu.VMEM((tm, 1), F32), pltpu.VMEM((tm, 1), F32)],
        compiler_params=_params("arbitrary", "arbitrary"),
        name="sgu_v",
    )(x, pre_g, mod, mod, w_in)


def _sgu_gate_kernel(h_ref, wu_ref, wz_ref, vg_ref, mu_ref, rstd_ref, ng_ref, ws_ref, bs_ref,
                     y_ref, *, groups_per_step):
    h = h_ref[...]
    tm = h.shape[0]
    nblk = tm // SGU_BLOCK
    u = _gelu(jnp.dot(h, wu_ref[...], preferred_element_type=F32))
    z = _silu(jnp.dot(h, wz_ref[...], preferred_element_type=F32))
    vn = ((vg_ref[...].astype(F32) - mu_ref[...]) * rstd_ref[...] * ng_ref[...]).astype(BF16)

    t_chunk = lax.broadcasted_iota(jnp.int32, (SGU_BLOCK, SGU_BLOCK), 0) // CHUNK
    s_chunk = lax.broadcasted_iota(jnp.int32, (SGU_BLOCK, SGU_BLOCK), 1) // CHUNK
    causal = s_chunk <= t_chunk
    for gi in range(groups_per_step):
        cols = slice(gi * SGU_GROUP_DIM, (gi + 1) * SGU_GROUP_DIM)
        w = jnp.where(causal, ws_ref[gi], 0.0).astype(BF16)
        vcat = jnp.concatenate(
            [vn[b * SGU_BLOCK:(b + 1) * SGU_BLOCK, cols] for b in range(nblk)], axis=1)
        vm = jnp.dot(w, vcat, preferred_element_type=F32) + bs_ref[gi]
        for b in range(nblk):
            rows = slice(b * SGU_BLOCK, (b + 1) * SGU_BLOCK)
            vmb = vm[:, b * SGU_GROUP_DIM:(b + 1) * SGU_GROUP_DIM]
            y_ref[rows, cols] = (u[rows, cols] * vmb * z[rows, cols]).astype(BF16)


def _sgu_gate(h, w_in, vg, mu, rstd, norm_g, w_s, b_s, seq):
    m, d = h.shape
    e = w_in.shape[1] // 3
    groups = e // SGU_GROUP_DIM
    gs = 2 if groups % 2 == 0 else 1
    tn = gs * SGU_GROUP_DIM
    tm = _tile(seq, 1024, SGU_BLOCK)
    return pl.pallas_call(
        functools.partial(_sgu_gate_kernel, groups_per_step=gs),
        grid=(m // tm, groups // gs),
        in_specs=[
            pl.BlockSpec((tm, d), lambda i, j: (i, 0)),
            pl.BlockSpec((d, tn), lambda i, j: (0, j)),
            pl.BlockSpec((d, tn), lambda i, j: (0, 2 * (e // tn) + j)),
            pl.BlockSpec((tm, tn), lambda i, j: (i, j)),
            pl.BlockSpec((tm, 1), lambda i, j: (i, 0)),
            pl.BlockSpec((tm, 1), lambda i, j: (i, 0)),
            pl.BlockSpec((1, tn), lambda i, j: (0, j)),
            pl.BlockSpec((gs, SGU_BLOCK, SGU_BLOCK), lambda i, j: (j, 0, 0)),
            pl.BlockSpec((gs, SGU_BLOCK, 1), lambda i, j: (j, 0, 0)),
        ],
        out_specs=pl.BlockSpec((tm, tn), lambda i, j: (i, j)),
        out_shape=jax.ShapeDtypeStruct((m, e), BF16),
        compiler_params=_params("arbitrary", "arbitrary"),
        name="sgu_gate",
    )(h, w_in, w_in, vg, mu, rstd, norm_g, w_s, b_s.reshape(groups, SGU_BLOCK, 1))


def _out_proj_kernel(y_ref, w_ref, x_ref, g_ref, gate_ref, o_ref):
    acc = jnp.dot(y_ref[...], w_ref[...], preferred_element_type=F32)
    ms = jnp.mean(acc * acc, axis=-1, keepdims=True)
    yn = acc * lax.rsqrt(ms + NORM_EPS) * g_ref[...]
    o_ref[...] = x_ref[...] + gate_ref[...] * yn


def _out_proj(y, w_out, x, post_g, mod, seq):
    m, d = x.shape
    k = w_out.shape[0]
    if y.ndim == 2:
        tm = _tile(seq, 512)
        y_spec = pl.BlockSpec((tm, k), lambda i: (i, 0))
    else:
        tm = y.shape[3]
        nq = seq // tm

        def pair_major(i):
            j = i % nq
            lower = j < nq // 2
            return (i // nq, jnp.where(lower, j, nq - 1 - j), jnp.where(lower, 0, 1), 0, 0)

        y_spec = pl.BlockSpec((None, None, None, tm, k), pair_major)
    _, _, gate_spec = _mod_specs(seq // tm, d)
    return pl.pallas_call(
        _out_proj_kernel,
        grid=(m // tm,),
        in_specs=[
            y_spec,
            pl.BlockSpec((k, d), lambda i: (0, 0), pipeline_mode=pl.Buffered(1)),
            pl.BlockSpec((tm, d), lambda i: (i, 0)),
            pl.BlockSpec((1, d), lambda i: (0, 0)),
            gate_spec,
        ],
        out_specs=pl.BlockSpec((tm, d), lambda i: (i, 0)),
        out_shape=jax.ShapeDtypeStruct((m, d), F32),
        compiler_params=_params("arbitrary"),
        name="out_proj",
    )(y, w_out, x, post_g, mod)


def _mla_in_kernel(x_ref, g_ref, shift_ref, scale_ref, w_ref, gq_ref, gkv_ref, cos_ref, sin_ref,
                   cq_ref, ckv_ref, kr_ref, sz_ref):
    h = _modulated_norm(x_ref[...], g_ref[...], shift_ref[...], scale_ref[...]).astype(BF16)
    nlat = 2 * LATENT_PAD + LANES
    lat = jnp.dot(h, w_ref[:, :nlat], preferred_element_type=F32)
    cq = lat[:, :LATENT_PAD]
    ckv = lat[:, LATENT_PAD:2 * LATENT_PAD]
    kr = lat[:, 2 * LATENT_PAD:]
    ms_q = jnp.sum(cq * cq, axis=-1, keepdims=True) * (1.0 / Q_LORA_RANK)
    cq_ref[...] = (cq * lax.rsqrt(ms_q + NORM_EPS) * gq_ref[...]).astype(BF16)
    ms_kv = jnp.mean(ckv * ckv, axis=-1, keepdims=True)
    ckv_ref[...] = (ckv * lax.rsqrt(ms_kv + NORM_EPS) * gkv_ref[...]).astype(BF16)
    kr_ref[...] = _rope_rotate(kr, cos_ref[...], sin_ref[...]).astype(BF16)
    z = jnp.dot(h, w_ref[:, nlat:], preferred_element_type=F32)
    sz_ref[...] = _silu(z).astype(BF16)


def _mla_in(x, pre_g, mod, w_in, gq, gkv, cos_t, sin_t, seq):
    m, d = x.shape
    ncols = w_in.shape[1]
    width = ncols - 2 * LATENT_PAD - LANES
    tm = _tile(seq, 512)
    tpb = seq // tm
    shift_spec, scale_spec, _ = _mod_specs(tpb, d)
    row = lambda n: pl.BlockSpec((tm, n), lambda i: (i, 0))
    const = lambda r, n: pl.BlockSpec((r, n), lambda i: (0, 0))
    table = pl.BlockSpec((tm, LANES), lambda i: (i % tpb, 0))
    return pl.pallas_call(
        _mla_in_kernel,
        grid=(m // tm,),
        in_specs=[
            row(d), const(1, d), shift_spec, scale_spec,
            pl.BlockSpec((d, ncols), lambda i: (0, 0), pipeline_mode=pl.Buffered(1)),
            const(1, LATENT_PAD), const(1, LATENT_PAD), table, table,
        ],
        out_specs=[row(LATENT_PAD), row(LATENT_PAD), row(LANES), row(width)],
        out_shape=[
            jax.ShapeDtypeStruct((m, LATENT_PAD), BF16),
            jax.ShapeDtypeStruct((m, LATENT_PAD), BF16),
            jax.ShapeDtypeStruct((m, LANES), BF16),
            jax.ShapeDtypeStruct((m, width), BF16),
        ],
        compiler_params=_params("arbitrary"),
        name="mla_in",
    )(x, pre_g, mod, mod, w_in, gq, gkv, cos_t, sin_t)


_NT_DIMS = (((1,), (1,)), ((), ()))


def _mla_up_kernel(cq_ref, ckv_ref, kr_ref, wqt_ref, wk_ref, wvt_ref, cos_ref, sin_ref,
                   qt_ref, k_ref, vt_ref, *, heads, scale, tk):
    ckv = ckv_ref[...]
    qt = lax.dot_general(wqt_ref[...], cq_ref[...], _NT_DIMS, preferred_element_type=F32) * scale
    cos_t = cos_ref[...]
    sin_t = sin_ref[...]
    half = QK_ROPE_DIM // 2
    for hd in range(heads):
        r0 = hd * HEAD_PAD
        r1 = r0 + HEAD_PAD - QK_ROPE_DIM
        qt_ref[r0:r1, :] = qt[r0:r1].astype(BF16)
        x1 = qt[r1:r1 + half]
        x2 = qt[r1 + half:r1 + QK_ROPE_DIM]
        qt_ref[r1:r1 + half, :] = (x1 * cos_t - x2 * sin_t).astype(BF16)
        qt_ref[r1 + half:r1 + QK_ROPE_DIM, :] = (x1 * sin_t + x2 * cos_t).astype(BF16)
    kn = jnp.dot(ckv, wk_ref[...], preferred_element_type=F32)
    kr = kr_ref[...]
    for hd in range(heads):
        c0 = hd * HEAD_PAD
        k_ref[:, c0:c0 + LANES] = kn[:, hd * QK_NOPE_DIM:(hd + 1) * QK_NOPE_DIM].astype(BF16)
        k_ref[:, c0 + LANES:c0 + HEAD_PAD] = kr
    vt = lax.dot_general(wvt_ref[...], ckv, _NT_DIMS, preferred_element_type=F32)
    for cb in range(vt.shape[1] // tk):
        vt_ref[cb] = vt[:, cb * tk:(cb + 1) * tk].astype(BF16)


def _mla_up(cq, ckv, kr, wqt, wk, wvt, cos_tt, sin_tt, seq, heads, tk):
    m = cq.shape[0]
    tm = _tile(seq, 512, tk)
    tpb = seq // tm
    row = lambda n: pl.BlockSpec((tm, n), lambda i: (i, 0))
    whole = lambda a: pl.BlockSpec(a.shape, lambda i: (0, 0), pipeline_mode=pl.Buffered(1))
    table = pl.BlockSpec((QK_ROPE_DIM // 2, tm), lambda i: (0, i % tpb))
    scale = float((QK_NOPE_DIM + QK_ROPE_DIM) ** -0.5 * 1.4426950408889634)
    return pl.pallas_call(
        functools.partial(_mla_up_kernel, heads=heads, scale=scale, tk=tk),
        grid=(m // tm,),
        in_specs=[row(LATENT_PAD), row(LATENT_PAD), row(LANES), whole(wqt), whole(wk), whole(wvt),
                  table, table],
        out_specs=[
            pl.BlockSpec((heads * HEAD_PAD, tm), lambda i: (0, i)),
            row(heads * HEAD_PAD),
            pl.BlockSpec((tm // tk, heads * V_HEAD_DIM, tk), lambda i: (i, 0, 0)),
        ],
        out_shape=[
            jax.ShapeDtypeStruct((heads * HEAD_PAD, m), BF16),
            jax.ShapeDtypeStruct((m, heads * HEAD_PAD), BF16),
            jax.ShapeDtypeStruct((m // tk, heads * V_HEAD_DIM, tk), BF16),
        ],
        compiler_params=_params("arbitrary"),
        name="mla_up",
    )(cq, ckv, kr, wqt, wk, wvt, cos_tt, sin_tt)


ATTN_KV_TILE = 256
ONES_ROWS = 16


def _attn_kernel(qa_ref, qb_ref, k_ref, vt_ref, sza_ref, szb_ref, o_ref, q_sc, m_sc, acc_sc,
                 *, tk, nq):
    i = pl.program_id(2)
    tq = 2 * tk
    ones = jnp.ones((ONES_ROWS, tk), BF16)
    q_sc[0] = qa_ref[...]
    q_sc[1] = qb_ref[...]

    def k_tile(t):
        return k_ref[pl.ds(pl.multiple_of(t * tk, tk), tk), :]

    def v_tile(t):
        return jnp.concatenate([vt_ref[t], ones], axis=0)

    m_sc[...] = jnp.full_like(m_sc, -jnp.inf)
    acc_sc[...] = jnp.zeros_like(acc_sc)

    visible = (lax.broadcasted_iota(jnp.int32, (tk, tk), 0) // CHUNK
               <= lax.broadcasted_iota(jnp.int32, (tk, tk), 1) // CHUNK)

    tiles = []
    for slot, blk in ((0, i), (1, nq - 1 - i)):
        tiles.append((slot, 2 * blk, 0, "left"))
        tiles.append((slot, 2 * blk + 1, tk, "all"))
    n_b = 2 * (nq - 1 - i)
    for n in range(2 * (nq - 1)):
        tiles.append((jnp.where(n < n_b, 1, 0), jnp.where(n < n_b, n, n - n_b), 0, None))

    def score_stage(tile):
        slot, t, lane0, mask = tile
        s = jnp.dot(k_tile(t), q_sc[slot][:, lane0:], preferred_element_type=F32)
        if mask == "left":
            s = jnp.concatenate([jnp.where(visible, s[:, :tk], -1e30), s[:, tk:]], axis=1)
        elif mask == "all":
            s = jnp.where(visible, s, -1e30)
        return s, jnp.max(s, axis=0, keepdims=True)

    def exp_stage(tile, s, s_max):
        slot, _, lane0, _ = tile
        lanes = pl.ds(lane0, tq - lane0)
        m_prev = m_sc[slot, :, lanes]
        m_new = jnp.maximum(m_prev, s_max)
        m_sc[slot, :, lanes] = m_new
        return jnp.exp2(s - m_new).astype(BF16), jnp.exp2(m_prev - m_new)

    def value_stage(tile, p, alpha):
        slot, t, lane0, _ = tile
        lanes = pl.ds(lane0, tq - lane0)
        acc_sc[slot, :, lanes] = (alpha * acc_sc[slot, :, lanes]
                                  + jnp.dot(v_tile(t), p, preferred_element_type=F32))

    scored = {}
    exped = {}
    for n in range(len(tiles) + 2):
        if n < len(tiles):
            scored[n] = score_stage(tiles[n])
        if 0 <= n - 1 < len(tiles):
            exped[n - 1] = exp_stage(tiles[n - 1], *scored.pop(n - 1))
        if 0 <= n - 2 < len(tiles):
            value_stage(tiles[n - 2], *exped.pop(n - 2))

    for slot, sz_ref in ((0, sza_ref), (1, szb_ref)):
        acc = acc_sc[slot]
        out_t = acc[:V_HEAD_DIM] / acc[V_HEAD_DIM:V_HEAD_DIM + 1]
        o_ref[slot] = (out_t.T * sz_ref[...].astype(F32)).astype(BF16)


def _attention(qt, k, vt, sz, batch, seq, heads, tk):
    tq = 2 * tk
    nq = seq // tq
    q_spec = lambda f: pl.BlockSpec((HEAD_PAD, tq), lambda b, h, i: (h, b * nq + f(i)))
    sz_spec = lambda f: pl.BlockSpec((tq, V_HEAD_DIM), lambda b, h, i: (b * nq + f(i), h))
    first = lambda i: i
    second = lambda i: nq - 1 - i
    return pl.pallas_call(
        functools.partial(_attn_kernel, tk=tk, nq=nq),
        grid=(batch, heads, nq // 2),
        in_specs=[
            q_spec(first), q_spec(second),
            pl.BlockSpec((seq, HEAD_PAD), lambda b, h, i: (b, h)),
            pl.BlockSpec((seq // tk, V_HEAD_DIM, tk), lambda b, h, i: (b, h, 0)),
            sz_spec(first), sz_spec(second),
        ],
        out_specs=pl.BlockSpec((None, None, 2, tq, V_HEAD_DIM), lambda b, h, i: (b, i, 0, 0, h)),
        out_shape=jax.ShapeDtypeStruct((batch, nq // 2, 2, tq, heads * V_HEAD_DIM), BF16),
        scratch_shapes=[
            pltpu.VMEM((2, HEAD_PAD, tq), BF16),
            pltpu.VMEM((2, 1, tq), F32),
            pltpu.VMEM((2, V_HEAD_DIM + ONES_ROWS, tq), F32),
        ],
        compiler_params=_params("arbitrary", "arbitrary", "arbitrary"),
        name="mla_attention",
    )(qt, qt, k, vt, sz, sz)


def _rope_tables(seq):
    pos = jnp.arange(seq, dtype=F32)
    inv_freq = ROPE_THETA ** (-jnp.arange(0, QK_ROPE_DIM, 2, dtype=F32) / QK_ROPE_DIM)
    ang = pos[:, None] * inv_freq[None, :]
    cos, sin = jnp.cos(ang), jnp.sin(ang)
    ones = jnp.ones((seq, ROPE_LANE0), F32)
    zeros = jnp.zeros((seq, ROPE_LANE0), F32)
    return (jnp.concatenate([ones, cos, cos], axis=-1),
            jnp.concatenate([zeros, -sin, sin], axis=-1), cos.T, sin.T)


def _mla_weight_layout(w_in, gq, w_uq, w_ukv, heads):
    nb, d, _ = w_in.shape
    o1, o2, o3 = Q_LORA_RANK, Q_LORA_RANK + KV_LORA_RANK, Q_LORA_RANK + KV_LORA_RANK + QK_ROPE_DIM
    zc = lambda n: jnp.zeros((nb, d, n), w_in.dtype)
    w_in_l = jnp.concatenate([
        w_in[..., :o1], zc(LATENT_PAD - Q_LORA_RANK),
        w_in[..., o1:o2], zc(LATENT_PAD - KV_LORA_RANK),
        zc(ROPE_LANE0), w_in[..., o2:o3],
        w_in[..., o3:],
    ], axis=-1).astype(BF16)
    gq_l = jnp.pad(gq, ((0, 0), (0, LATENT_PAD - Q_LORA_RANK)))
    wq = w_uq.reshape(nb, Q_LORA_RANK, heads, QK_NOPE_DIM + QK_ROPE_DIM)
    wq = jnp.concatenate([
        wq[..., :QK_NOPE_DIM],
        jnp.zeros((nb, Q_LORA_RANK, heads, ROPE_LANE0), wq.dtype),
        wq[..., QK_NOPE_DIM:],
    ], axis=-1).reshape(nb, Q_LORA_RANK, heads * HEAD_PAD)
    wqt_l = jnp.swapaxes(
        jnp.pad(wq, ((0, 0), (0, LATENT_PAD - Q_LORA_RANK), (0, 0))), 1, 2).astype(BF16)
    wkv = w_ukv.reshape(nb, KV_LORA_RANK, heads, QK_NOPE_DIM + V_HEAD_DIM)
    wk_l = wkv[..., :QK_NOPE_DIM].reshape(nb, KV_LORA_RANK, heads * QK_NOPE_DIM).astype(BF16)
    wvt_l = jnp.swapaxes(
        wkv[..., QK_NOPE_DIM:].reshape(nb, KV_LORA_RANK, heads * V_HEAD_DIM), 1, 2).astype(BF16)
    return w_in_l, gq_l, wqt_l, wk_l, wvt_l


def kernel(x, c, ada_w, ada_b, pre_g, post_g, sgu_w_in, sgu_norm_g, sgu_w_s, sgu_b_s, sgu_w_out,
           mla_w_in, mla_q_norm_g, mla_kv_norm_g, mla_w_uq, mla_w_ukv, mla_w_out):
    batch, seq, d = x.shape
    depth = ada_w.shape[0]
    heads = mla_w_out.shape[1] // V_HEAD_DIM
    assert seq % SGU_BLOCK == 0 and d % LANES == 0
    assert sgu_w_in.shape[2] % (3 * SGU_GROUP_DIM) == 0
    assert mla_kv_norm_g.shape[1] == KV_LORA_RANK == LATENT_PAD

    mod = _adaln_mod(c, ada_w, ada_b).reshape(depth, batch, 3, 1, d)
    sgu_w_in_l = sgu_w_in.astype(BF16)
    sgu_w_out_l = sgu_w_out.astype(BF16)
    mla_w_in_l, gq_l, wqt_l, wk_l, wvt_l = _mla_weight_layout(
        mla_w_in, mla_q_norm_g, mla_w_uq, mla_w_ukv, heads)
    mla_w_out_l = mla_w_out.astype(BF16)
    cos_t, sin_t, cos_tt, sin_tt = _rope_tables(seq)
    tk = ATTN_KV_TILE
    assert seq % (2 * tk) == 0

    xf = x.reshape(batch * seq, d)
    for i in range(depth):
        j = i // 2
        pg = pre_g[i][None, :]
        if i % 2 == 0:
            h, vg, mu, rstd = _sgu_v(xf, pg, mod[i], sgu_w_in_l[j], seq)
            y = _sgu_gate(h, sgu_w_in_l[j], vg, mu, rstd, sgu_norm_g[j][None, :], sgu_w_s[j],
                          sgu_b_s[j], seq)
            w_out = sgu_w_out_l[j]
        else:
            cq, ckv, kr, sz = _mla_in(xf, pg, mod[i], mla_w_in_l[j], gq_l[j][None, :],
                                      mla_kv_norm_g[j][None, :], cos_t, sin_t, seq)
            qt, k, vt = _mla_up(cq, ckv, kr, wqt_l[j], wk_l[j], wvt_l[j], cos_tt, sin_tt, seq, heads, tk)
            y = _attention(qt, k, vt, sz, batch, seq, heads, tk)
            w_out = mla_w_out_l[j]
        xf = _out_proj(y, w_out, xf, post_g[i][None, :], mod[i], seq)
    return xf.reshape(batch, seq, d)
```

```python
import functools

import jax
import jax.numpy as jnp
from jax import lax
from jax.experimental import pallas as pl
from jax.experimental.pallas import tpu as pltpu

F32 = jnp.float32
BF16 = jnp.bfloat16

NORM_EPS = 1e-6
CHUNK = 64
SGU_BLOCK = 128
SGU_GROUP_DIM = 256
Q_LORA_RANK = 448
KV_LORA_RANK = 512
QK_NOPE_DIM = 128
QK_ROPE_DIM = 64
V_HEAD_DIM = 128
ROPE_THETA = 10000.0
LANES = 128
LATENT_PAD = 512
HEAD_PAD = 2 * LANES
ROPE_LANE0 = LANES - QK_ROPE_DIM
ATTN_KV_TILE = 256
ONES_ROWS = 16
VMEM_LIMIT_BYTES = 60 * 1024 * 1024
_NN_DIMS = (((1,), (0,)), ((), ()))
_NT_DIMS = (((1,), (1,)), ((), ()))
_TN_DIMS = (((0,), (0,)), ((), ()))


def _tile(n, pref, mult=8):
    t = min(n, pref)
    while t > mult and (n % t or t % mult):
        t -= mult
    return t if n % t == 0 else n


def _params(*semantics):
    return pltpu.CompilerParams(dimension_semantics=semantics, vmem_limit_bytes=VMEM_LIMIT_BYTES)


def _gelu(x):
    return 0.5 * x * (1.0 + lax.erf(x * (0.5 ** 0.5)))


def _silu(x):
    return x * jax.nn.sigmoid(x)


def _modulated_norm(x, g, shift, scale):
    ms = jnp.mean(x * x, axis=-1, keepdims=True)
    y = x * lax.rsqrt(ms + NORM_EPS) * g
    return y * (1.0 + scale) + shift


def _rope_rotate(x, cos_t, sin_t):
    lane = lax.broadcasted_iota(jnp.int32, x.shape, 1)
    half = QK_ROPE_DIM // 2
    partner = jnp.where(lane < ROPE_LANE0 + half,
                        pltpu.roll(x, LANES - half, 1),
                        pltpu.roll(x, half, 1))
    return x * cos_t + partner * sin_t


ROW_BLOCK = 256


def _pipelined(n_blocks, produce, consume):
    pending = None
    for r in range(n_blocks):
        current = produce(r)
        if pending is not None:
            consume(r - 1, pending)
        pending = current
    consume(n_blocks - 1, pending)


def _layer_spec(layer, rows, cols):
    return pl.BlockSpec((None, rows, cols), lambda *_: (layer, 0, 0), pipeline_mode=pl.Buffered(1))


def _mod_specs(layer, tiles_per_batch, d):
    def spec(k):
        return pl.BlockSpec((None, None, None, 1, d),
                            lambda m, *_: (layer, m // tiles_per_batch, k, 0, 0))
    return spec(0), spec(1), spec(2)


def _mod_kernel(c_ref, w_ref, b_ref, o_ref):
    cond = _silu(c_ref[...]).astype(BF16)
    o_ref[0] = jnp.dot(cond, w_ref[0].astype(BF16), preferred_element_type=F32) + b_ref[0]


def _adaln_mod(c, ada_w, ada_b):
    depth, d, n = ada_w.shape
    b = c.shape[0]
    tn = _tile(n, 1536, LANES)
    return pl.pallas_call(
        _mod_kernel,
        grid=(depth, n // tn),
        in_specs=[
            pl.BlockSpec((b, d), lambda i, j: (0, 0)),
            pl.BlockSpec((1, d, tn), lambda i, j: (i, 0, j)),
            pl.BlockSpec((1, 1, tn), lambda i, j: (i, 0, j)),
        ],
        out_specs=pl.BlockSpec((1, b, tn), lambda i, j: (i, 0, j)),
        out_shape=jax.ShapeDtypeStruct((depth, b, n), F32),
        compiler_params=_params("arbitrary", "arbitrary"),
        name="adaln_mod",
    )(c, ada_w, ada_b.reshape(depth, 1, n))


def _pre_norm_kernel(x_ref, g_ref, shift_ref, scale_ref, h_ref):
    h_ref[...] = _modulated_norm(x_ref[...], g_ref[...], shift_ref[...], scale_ref[...]).astype(BF16)


def _pre_norm(x, pre_g, mod, layer, seq):
    m, d = x.shape
    tm = _tile(seq, 512)
    shift_spec, scale_spec, _ = _mod_specs(layer, seq // tm, d)
    return pl.pallas_call(
        _pre_norm_kernel,
        grid=(m // tm,),
        in_specs=[pl.BlockSpec((tm, d), lambda i: (i, 0)), _layer_spec(layer, 1, d),
                  shift_spec, scale_spec],
        out_specs=pl.BlockSpec((tm, d), lambda i: (i, 0)),
        out_shape=jax.ShapeDtypeStruct((m, d), BF16),
        compiler_params=_params("arbitrary"),
        name="pre_norm",
    )(x, pre_g, mod, mod)


def _sgu_v_kernel(h_ref, w_ref, vg_ref, mu_ref, rstd_ref, s1_ref, s2_ref, *, width):
    n = pl.program_id(1)

    @pl.when(n == 0)
    def _():
        s1_ref[...] = jnp.zeros_like(s1_ref)
        s2_ref[...] = jnp.zeros_like(s2_ref)

    tr = min(ROW_BLOCK, h_ref.shape[0])

    def project(r):
        return jnp.dot(h_ref[r * tr:(r + 1) * tr, :], w_ref[...], preferred_element_type=F32)

    def activate(r, acc):
        rows = slice(r * tr, (r + 1) * tr)
        v = _gelu(acc)
        vg_ref[rows, :] = v.astype(BF16)
        s1_ref[rows, :] += jnp.sum(v, axis=-1, keepdims=True)
        s2_ref[rows, :] += jnp.sum(v * v, axis=-1, keepdims=True)

    _pipelined(h_ref.shape[0] // tr, project, activate)

    @pl.when(n == pl.num_programs(1) - 1)
    def _():
        mu = s1_ref[...] * (1.0 / width)
        var = s2_ref[...] * (1.0 / width) - mu * mu
        mu_ref[...] = mu
        rstd_ref[...] = lax.rsqrt(var + NORM_EPS)


def _sgu_v(h, w_in, layer, seq):
    m, d = h.shape
    e = w_in.shape[2] // 3
    tm = _tile(seq, 1024, SGU_BLOCK)
    tn = _tile(e, 2048, LANES)
    return pl.pallas_call(
        functools.partial(_sgu_v_kernel, width=e),
        grid=(m // tm, e // tn),
        in_specs=[
            pl.BlockSpec((tm, d), lambda i, j: (i, 0)),
            pl.BlockSpec((None, d, tn), lambda i, j: (layer, 0, e // tn + j)),
        ],
        out_specs=[
            pl.BlockSpec((tm, tn), lambda i, j: (i, j)),
            pl.BlockSpec((tm, 1), lambda i, j: (i, 0)),
            pl.BlockSpec((tm, 1), lambda i, j: (i, 0)),
        ],
        out_shape=[
            jax.ShapeDtypeStruct((m, e), BF16),
            jax.ShapeDtypeStruct((m, 1), F32),
            jax.ShapeDtypeStruct((m, 1), F32),
        ],
        scratch_shapes=[pltpu.VMEM((tm, 1), F32), pltpu.VMEM((tm, 1), F32)],
        compiler_params=_params("arbitrary", "arbitrary"),
        name="sgu_v",
    )(h, w_in)


def _sgu_gate_kernel(h_ref, wu_ref, wz_ref, vg_ref, mu_ref, rstd_ref, ng_ref, ws_ref, bs_ref,
                     y_ref, *, groups_per_step):
    tm = h_ref.shape[0]
    tr = min(ROW_BLOCK, tm)
    nblk = tr // SGU_BLOCK
    t_chunk = lax.broadcasted_iota(jnp.int32, (SGU_BLOCK, SGU_BLOCK), 0) // CHUNK
    s_chunk = lax.broadcasted_iota(jnp.int32, (SGU_BLOCK, SGU_BLOCK), 1) // CHUNK
    causal = s_chunk <= t_chunk
    ws = [jnp.where(causal, ws_ref[gi], 0.0).astype(BF16) for gi in range(groups_per_step)]

    def project(r):
        rows = slice(r * tr, (r + 1) * tr)
        h = h_ref[rows, :]
        vn = ((vg_ref[rows, :].astype(F32) - mu_ref[rows, :]) * rstd_ref[rows, :]
              * ng_ref[...]).astype(BF16)
        vms = []
        for gi in range(groups_per_step):
            cols = slice(gi * SGU_GROUP_DIM, (gi + 1) * SGU_GROUP_DIM)
            vcat = jnp.concatenate(
                [vn[b * SGU_BLOCK:(b + 1) * SGU_BLOCK, cols] for b in range(nblk)], axis=1)
            vms.append(jnp.dot(ws[gi], vcat, preferred_element_type=F32))
        return (jnp.dot(h, wu_ref[...], preferred_element_type=F32),
                jnp.dot(h, wz_ref[...], preferred_element_type=F32), vms)

    def gate(r, projected):
        u_acc, z_acc, vms = projected
        u = _gelu(u_acc)
        z = _silu(z_acc)
        for gi in range(groups_per_step):
            cols = slice(gi * SGU_GROUP_DIM, (gi + 1) * SGU_GROUP_DIM)
            vm = vms[gi] + bs_ref[gi]
            for b in range(nblk):
                rows = slice(b * SGU_BLOCK, (b + 1) * SGU_BLOCK)
                vmb = vm[:, b * SGU_GROUP_DIM:(b + 1) * SGU_GROUP_DIM]
                y_ref[r * tr + b * SGU_BLOCK:r * tr + (b + 1) * SGU_BLOCK, cols] = (
                    u[rows, cols] * vmb * z[rows, cols]).astype(BF16)

    _pipelined(tm // tr, project, gate)


def _sgu_gate(h, w_in, vg, mu, rstd, norm_g, w_s, b_s, layer, seq):
    m, d = h.shape
    e = w_in.shape[2] // 3
    groups = e // SGU_GROUP_DIM
    gs = next(g for g in (4, 2, 1) if groups % g == 0)
    tn = gs * SGU_GROUP_DIM
    tm = _tile(seq, 1024, SGU_BLOCK)
    return pl.pallas_call(
        functools.partial(_sgu_gate_kernel, groups_per_step=gs),
        grid=(m // tm, groups // gs),
        in_specs=[
            pl.BlockSpec((tm, d), lambda i, j: (i, 0)),
            pl.BlockSpec((None, d, tn), lambda i, j: (layer, 0, j)),
            pl.BlockSpec((None, d, tn), lambda i, j: (layer, 0, 2 * (e // tn) + j)),
            pl.BlockSpec((tm, tn), lambda i, j: (i, j)),
            pl.BlockSpec((tm, 1), lambda i, j: (i, 0)),
            pl.BlockSpec((tm, 1), lambda i, j: (i, 0)),
            pl.BlockSpec((None, 1, tn), lambda i, j: (layer, 0, j)),
            pl.BlockSpec((None, gs, SGU_BLOCK, SGU_BLOCK), lambda i, j: (layer, j, 0, 0)),
            pl.BlockSpec((None, gs, SGU_BLOCK, 1), lambda i, j: (layer, j, 0, 0)),
        ],
        out_specs=pl.BlockSpec((tm, tn), lambda i, j: (i, j)),
        out_shape=jax.ShapeDtypeStruct((m, e), BF16),
        compiler_params=_params("arbitrary", "arbitrary"),
        name="sgu_gate",
    )(h, w_in, w_in, vg, mu, rstd, norm_g, w_s, b_s)


def _out_proj_kernel(y_ref, w_ref, x_ref, g_ref, gate_ref, *rest, y_transposed, emit_h):
    if emit_h:
        next_g_ref, next_shift_ref, next_scale_ref, o_ref, h_ref = rest
    else:
        (o_ref,) = rest
    tm = o_ref.shape[0]
    tr = min(ROW_BLOCK, tm)

    def project(r):
        if y_transposed:
            return lax.dot_general(y_ref[:, r * tr:(r + 1) * tr], w_ref[...], _TN_DIMS,
                                   preferred_element_type=F32)
        return jnp.dot(y_ref[r * tr:(r + 1) * tr, :], w_ref[...], preferred_element_type=F32)

    def residual(r, acc):
        rows = slice(r * tr, (r + 1) * tr)
        ms = jnp.mean(acc * acc, axis=-1, keepdims=True)
        yn = acc * lax.rsqrt(ms + NORM_EPS) * g_ref[...]
        x_new = x_ref[rows, :] + gate_ref[...] * yn
        o_ref[rows, :] = x_new
        if emit_h:
            h_ref[rows, :] = _modulated_norm(x_new, next_g_ref[...], next_shift_ref[...],
                                             next_scale_ref[...]).astype(BF16)

    _pipelined(tm // tr, project, residual)


def _out_proj(y, w_out, mixer_layer, x, post_g, pre_g, mod, layer, seq, emit_h):
    m, d = x.shape
    k = w_out.shape[1]
    if y.ndim == 2:
        tm = _tile(seq, 512)
        y_spec = pl.BlockSpec((tm, k), lambda i: (i, 0))
    else:
        tm = y.shape[4]
        nq = seq // tm

        def pair_major(i):
            j = i % nq
            lower = j < nq // 2
            return (i // nq, jnp.where(lower, j, nq - 1 - j), jnp.where(lower, 0, 1), 0, 0)

        y_spec = pl.BlockSpec((None, None, None, k, tm), pair_major)
    tpb = seq // tm
    row = pl.BlockSpec((tm, d), lambda i: (i, 0))
    _, _, gate_spec = _mod_specs(layer, tpb, d)
    in_specs = [y_spec, _layer_spec(mixer_layer, k, d), row, _layer_spec(layer, 1, d), gate_spec]
    args = [y, w_out, x, post_g, mod]
    out_specs = [row]
    out_shape = [jax.ShapeDtypeStruct((m, d), F32)]
    if emit_h:
        next_shift_spec, next_scale_spec, _ = _mod_specs(layer + 1, tpb, d)
        in_specs += [_layer_spec(layer + 1, 1, d), next_shift_spec, next_scale_spec]
        args += [pre_g, mod, mod]
        out_specs.append(row)
        out_shape.append(jax.ShapeDtypeStruct((m, d), BF16))
    outs = pl.pallas_call(
        functools.partial(_out_proj_kernel, y_transposed=y.ndim != 2, emit_h=emit_h),
        grid=(m // tm,),
        in_specs=in_specs,
        out_specs=out_specs,
        out_shape=out_shape,
        compiler_params=_params("arbitrary"),
        name="out_proj",
    )(*args)
    return (outs[0], outs[1]) if emit_h else (outs[0], None)


def _mla_in_kernel(x_ref, g_ref, shift_ref, scale_ref, w_ref, wzt_ref, gq_ref, gkv_ref, cos_ref,
                   sin_ref, cq_ref, ckv_ref, kr_ref, szt_ref):
    h = _modulated_norm(x_ref[...], g_ref[...], shift_ref[...], scale_ref[...]).astype(BF16)
    lat = jnp.dot(h, w_ref[...], preferred_element_type=F32)
    cq = lat[:, :LATENT_PAD]
    ckv = lat[:, LATENT_PAD:2 * LATENT_PAD]
    kr = lat[:, 2 * LATENT_PAD:]
    ms_q = jnp.sum(cq * cq, axis=-1, keepdims=True) * (1.0 / Q_LORA_RANK)
    cq_ref[...] = (cq * lax.rsqrt(ms_q + NORM_EPS) * gq_ref[...]).astype(BF16)
    ms_kv = jnp.mean(ckv * ckv, axis=-1, keepdims=True)
    ckv_ref[...] = (ckv * lax.rsqrt(ms_kv + NORM_EPS) * gkv_ref[...]).astype(BF16)
    kr_ref[...] = _rope_rotate(kr, cos_ref[...], sin_ref[...]).astype(BF16)
    zt = lax.dot_general(wzt_ref[...], h, _NT_DIMS, preferred_element_type=F32)
    szt_ref[...] = _silu(zt).astype(BF16)


def _mla_in(x, pre_g, mod, w_lat, wzt, gq, gkv, cos_t, sin_t, layer, mixer_layer, seq):
    m, d = x.shape
    nlat = w_lat.shape[2]
    width = wzt.shape[1]
    tm = _tile(seq, 512)
    tpb = seq // tm
    row = lambda n: pl.BlockSpec((tm, n), lambda i: (i, 0))
    table = pl.BlockSpec((tm, LANES), lambda i: (i % tpb, 0))
    shift_spec, scale_spec, _ = _mod_specs(layer, tpb, d)
    j = mixer_layer
    return pl.pallas_call(
        _mla_in_kernel,
        grid=(m // tm,),
        in_specs=[
            row(d), _layer_spec(layer, 1, d), shift_spec, scale_spec,
            _layer_spec(j, d, nlat), _layer_spec(j, width, d),
            _layer_spec(j, 1, LATENT_PAD), _layer_spec(j, 1, LATENT_PAD), table, table,
        ],
        out_specs=[row(LATENT_PAD), row(LATENT_PAD), row(LANES),
                   pl.BlockSpec((width, tm), lambda i: (0, i))],
        out_shape=[
            jax.ShapeDtypeStruct((m, LATENT_PAD), BF16),
            jax.ShapeDtypeStruct((m, LATENT_PAD), BF16),
            jax.ShapeDtypeStruct((m, LANES), BF16),
            jax.ShapeDtypeStruct((width, m), BF16),
        ],
        compiler_params=_params("arbitrary"),
        name="mla_in",
    )(x, pre_g, mod, mod, w_lat, wzt, gq, gkv, cos_t, sin_t)


def _mla_up_kernel(cq_ref, ckv_ref, kr_ref, wqt_ref, wk_ref, wvt_ref, cos_ref, sin_ref,
                   qt_ref, k_ref, vt_ref, *, heads, scale, tk):
    ckv = ckv_ref[...]
    qt = lax.dot_general(wqt_ref[...], cq_ref[...], _NT_DIMS, preferred_element_type=F32) * scale
    cos_t = cos_ref[...]
    sin_t = sin_ref[...]
    half = QK_ROPE_DIM // 2
    for hd in range(heads):
        r0 = hd * HEAD_PAD
        r1 = r0 + HEAD_PAD - QK_ROPE_DIM
        qt_ref[r0:r1, :] = qt[r0:r1].astype(BF16)
        x1 = qt[r1:r1 + half]
        x2 = qt[r1 + half:r1 + QK_ROPE_DIM]
        qt_ref[r1:r1 + half, :] = (x1 * cos_t - x2 * sin_t).astype(BF16)
        qt_ref[r1 + half:r1 + QK_ROPE_DIM, :] = (x1 * sin_t + x2 * cos_t).astype(BF16)
    kn = jnp.dot(ckv, wk_ref[...], preferred_element_type=F32)
    kr = kr_ref[...]
    for hd in range(heads):
        c0 = hd * HEAD_PAD
        k_ref[:, c0:c0 + LANES] = kn[:, hd * QK_NOPE_DIM:(hd + 1) * QK_NOPE_DIM].astype(BF16)
        k_ref[:, c0 + LANES:c0 + HEAD_PAD] = kr
    vt = lax.dot_general(wvt_ref[...], ckv, _NT_DIMS, preferred_element_type=F32)
    for cb in range(vt.shape[1] // tk):
        vt_ref[cb] = vt[:, cb * tk:(cb + 1) * tk].astype(BF16)


def _mla_up(cq, ckv, kr, wqt, wk, wvt, cos_tt, sin_tt, layer, seq, heads, tk):
    m = cq.shape[0]
    tm = _tile(seq, 512, tk)
    tpb = seq // tm
    row = lambda n: pl.BlockSpec((tm, n), lambda i: (i, 0))
    whole = lambda a: _layer_spec(layer, a.shape[1], a.shape[2])
    table = pl.BlockSpec((QK_ROPE_DIM // 2, tm), lambda i: (0, i % tpb))
    scale = float((QK_NOPE_DIM + QK_ROPE_DIM) ** -0.5 * 1.4426950408889634)
    return pl.pallas_call(
        functools.partial(_mla_up_kernel, heads=heads, scale=scale, tk=tk),
        grid=(m // tm,),
        in_specs=[row(LATENT_PAD), row(LATENT_PAD), row(LANES), whole(wqt), whole(wk), whole(wvt),
                  table, table],
        out_specs=[
            pl.BlockSpec((heads * HEAD_PAD, tm), lambda i: (0, i)),
            row(heads * HEAD_PAD),
            pl.BlockSpec((tm // tk, heads * V_HEAD_DIM, tk), lambda i: (i, 0, 0)),
        ],
        out_shape=[
            jax.ShapeDtypeStruct((heads * HEAD_PAD, m), BF16),
            jax.ShapeDtypeStruct((m, heads * HEAD_PAD), BF16),
            jax.ShapeDtypeStruct((m // tk, heads * V_HEAD_DIM, tk), BF16),
        ],
        compiler_params=_params("arbitrary"),
        name="mla_up",
    )(cq, ckv, kr, wqt, wk, wvt, cos_tt, sin_tt)


def _attn_kernel(qa_ref, qb_ref, k_ref, vt_ref, sza_ref, szb_ref, o_ref, m_sc, acc_sc, *, tk, nq):
    i = pl.program_id(2)
    tq = 2 * tk
    ones = jnp.ones((ONES_ROWS, tk), BF16)
    q_refs = (qa_ref, qb_ref)
    sz_refs = (sza_ref, szb_ref)

    def k_tile(t):
        return k_ref[t * tk:(t + 1) * tk, :]

    def v_tile(t):
        return jnp.concatenate([vt_ref[t], ones], axis=0)

    def masked(s):
        visible = (lax.broadcasted_iota(jnp.int32, (tk, tk), 0) // CHUNK
                   <= lax.broadcasted_iota(jnp.int32, (tk, tk), 1) // CHUNK)
        return jnp.where(visible, s, -1e30)

    def score_stage(tile):
        slot, t, lane0, mask = tile
        s = jnp.dot(k_tile(t), q_refs[slot][:, lane0:], preferred_element_type=F32)
        if mask == "left":
            s = jnp.concatenate([masked(s[:, :tk]), s[:, tk:]], axis=1)
        elif mask == "all":
            s = masked(s)
        return s, jnp.max(s, axis=0, keepdims=True)

    def exp_stage(tile, s, s_max):
        slot, _, lane0, mask = tile
        if mask == "left":
            m_sc[slot] = s_max
            return jnp.exp2(s - s_max).astype(BF16), None
        lanes = pl.ds(lane0, tq - lane0)
        m_prev = m_sc[slot, :, lanes]
        m_new = jnp.maximum(m_prev, s_max)
        m_sc[slot, :, lanes] = m_new
        return jnp.exp2(s - m_new).astype(BF16), jnp.exp2(m_prev - m_new)

    def value_stage(tile, p, alpha):
        slot, t, lane0, _ = tile
        pv = jnp.dot(v_tile(t), p, preferred_element_type=F32)
        if alpha is None:
            acc_sc[slot] = pv
        else:
            lanes = pl.ds(lane0, tq - lane0)
            acc_sc[slot, :, lanes] = alpha * acc_sc[slot, :, lanes] + pv

    def finalize(slot):
        acc = acc_sc[slot]
        out_t = acc[:V_HEAD_DIM] * (1.0 / acc[V_HEAD_DIM:V_HEAD_DIM + 1])
        o_ref[slot] = (out_t * sz_refs[slot][...].astype(F32)).astype(BF16)

    def run_pair(c):
        tiles = []
        last = {}
        for slot, blk in ((1, nq - 1 - c), (0, c)):
            tiles.append((slot, 2 * blk, 0, "left"))
            tiles.append((slot, 2 * blk + 1, tk, "all"))
            tiles.extend((slot, t, 0, None) for t in range(2 * blk))
            last[len(tiles) - 1] = slot
        scored = {}
        exped = {}
        for n in range(len(tiles) + 2):
            if n < len(tiles):
                scored[n] = score_stage(tiles[n])
            if 0 <= n - 1 < len(tiles):
                exped[n - 1] = exp_stage(tiles[n - 1], *scored.pop(n - 1))
            if 0 <= n - 2 < len(tiles):
                value_stage(tiles[n - 2], *exped.pop(n - 2))
                if n - 2 in last:
                    finalize(last[n - 2])

    for c in range(nq // 2):
        pl.when(i == c)(functools.partial(run_pair, c))


def _attention(qt, k, vt, szt, batch, seq, heads, tk):
    tq = 2 * tk
    nq = seq // tq
    q_spec = lambda f: pl.BlockSpec((HEAD_PAD, tq), lambda b, h, i: (h, b * nq + f(i)))
    sz_spec = lambda f: pl.BlockSpec((V_HEAD_DIM, tq), lambda b, h, i: (h, b * nq + f(i)))
    first = lambda i: i
    second = lambda i: nq - 1 - i
    return pl.pallas_call(
        functools.partial(_attn_kernel, tk=tk, nq=nq),
        grid=(batch, heads, nq // 2),
        in_specs=[
            q_spec(first), q_spec(second),
            pl.BlockSpec((seq, HEAD_PAD), lambda b, h, i: (b, h)),
            pl.BlockSpec((seq // tk, V_HEAD_DIM, tk), lambda b, h, i: (b, h, 0)),
            sz_spec(first), sz_spec(second),
        ],
        out_specs=pl.BlockSpec((None, None, 2, V_HEAD_DIM, tq), lambda b, h, i: (b, i, 0, h, 0)),
        out_shape=jax.ShapeDtypeStruct((batch, nq // 2, 2, heads * V_HEAD_DIM, tq), BF16),
        scratch_shapes=[
            pltpu.VMEM((2, 1, tq), F32),
            pltpu.VMEM((2, V_HEAD_DIM + ONES_ROWS, tq), F32),
        ],
        compiler_params=_params("arbitrary", "arbitrary", "arbitrary"),
        name="mla_attention",
    )(qt, qt, k, vt, szt, szt)


def _rope_tables(seq):
    pos = jnp.arange(seq, dtype=F32)
    inv_freq = ROPE_THETA ** (-jnp.arange(0, QK_ROPE_DIM, 2, dtype=F32) / QK_ROPE_DIM)
    ang = pos[:, None] * inv_freq[None, :]
    cos, sin = jnp.cos(ang), jnp.sin(ang)
    ones = jnp.ones((seq, ROPE_LANE0), F32)
    zeros = jnp.zeros((seq, ROPE_LANE0), F32)
    return (jnp.concatenate([ones, cos, cos], axis=-1),
            jnp.concatenate([zeros, -sin, sin], axis=-1), cos.T, sin.T)


def _mla_weight_layout(w_in, gq, w_uq, w_ukv, heads):
    nb, d, _ = w_in.shape
    o1, o2, o3 = Q_LORA_RANK, Q_LORA_RANK + KV_LORA_RANK, Q_LORA_RANK + KV_LORA_RANK + QK_ROPE_DIM
    zc = lambda n: jnp.zeros((nb, d, n), w_in.dtype)
    w_lat = jnp.concatenate([
        w_in[..., :o1], zc(LATENT_PAD - Q_LORA_RANK),
        w_in[..., o1:o2], zc(LATENT_PAD - KV_LORA_RANK),
        zc(ROPE_LANE0), w_in[..., o2:o3],
    ], axis=-1).astype(BF16)
    wzt = jnp.swapaxes(w_in[..., o3:], 1, 2).astype(BF16)
    gq_l = jnp.pad(gq, ((0, 0), (0, LATENT_PAD - Q_LORA_RANK)))[:, None, :]
    wq = w_uq.reshape(nb, Q_LORA_RANK, heads, QK_NOPE_DIM + QK_ROPE_DIM)
    wq = jnp.concatenate([
        wq[..., :QK_NOPE_DIM],
        jnp.zeros((nb, Q_LORA_RANK, heads, ROPE_LANE0), wq.dtype),
        wq[..., QK_NOPE_DIM:],
    ], axis=-1).reshape(nb, Q_LORA_RANK, heads * HEAD_PAD)
    wqt = jnp.swapaxes(
        jnp.pad(wq, ((0, 0), (0, LATENT_PAD - Q_LORA_RANK), (0, 0))), 1, 2).astype(BF16)
    wkv = w_ukv.reshape(nb, KV_LORA_RANK, heads, QK_NOPE_DIM + V_HEAD_DIM)
    wk = wkv[..., :QK_NOPE_DIM].reshape(nb, KV_LORA_RANK, heads * QK_NOPE_DIM).astype(BF16)
    wvt = jnp.swapaxes(
        wkv[..., QK_NOPE_DIM:].reshape(nb, KV_LORA_RANK, heads * V_HEAD_DIM), 1, 2).astype(BF16)
    return w_lat, wzt, gq_l, wqt, wk, wvt


def kernel(x, c, ada_w, ada_b, pre_g, post_g, sgu_w_in, sgu_norm_g, sgu_w_s, sgu_b_s, sgu_w_out,
           mla_w_in, mla_q_norm_g, mla_kv_norm_g, mla_w_uq, mla_w_ukv, mla_w_out):
    batch, seq, d = x.shape
    depth = ada_w.shape[0]
    heads = mla_w_out.shape[1] // V_HEAD_DIM
    tk = ATTN_KV_TILE
    assert seq % SGU_BLOCK == 0 and seq % (2 * tk) == 0 and d % LANES == 0
    assert sgu_w_in.shape[2] % (3 * SGU_GROUP_DIM) == 0
    assert mla_kv_norm_g.shape[1] == KV_LORA_RANK == LATENT_PAD

    mod = _adaln_mod(c, ada_w, ada_b).reshape(depth, batch, 3, 1, d)
    pre_g3 = pre_g[:, None, :]
    post_g3 = post_g[:, None, :]
    sgu_w_in_l = sgu_w_in.astype(BF16)
    sgu_w_out_l = sgu_w_out.astype(BF16)
    sgu_norm_g3 = sgu_norm_g[:, None, :]
    sgu_b_s4 = sgu_b_s[..., None]
    w_lat, wzt, gq_l, wqt, wk, wvt = _mla_weight_layout(
        mla_w_in, mla_q_norm_g, mla_w_uq, mla_w_ukv, heads)
    gkv_l = mla_kv_norm_g[:, None, :]
    mla_w_out_l = mla_w_out.astype(BF16)
    cos_t, sin_t, cos_tt, sin_tt = _rope_tables(seq)

    xf = x.reshape(batch * seq, d)
    h = _pre_norm(xf, pre_g3, mod, 0, seq)
    for i in range(depth):
        j = i // 2
        if i % 2 == 0:
            vg, mu, rstd = _sgu_v(h, sgu_w_in_l, j, seq)
            y = _sgu_gate(h, sgu_w_in_l, vg, mu, rstd, sgu_norm_g3, sgu_w_s, sgu_b_s4, j, seq)
            w_out = sgu_w_out_l
        else:
            cq, ckv, kr, szt = _mla_in(xf, pre_g3, mod, w_lat, wzt, gq_l, gkv_l, cos_t, sin_t,
                                       i, j, seq)
            qt, k, vt = _mla_up(cq, ckv, kr, wqt, wk, wvt, cos_tt, sin_tt, j, seq, heads, tk)
            y = _attention(qt, k, vt, szt, batch, seq, heads, tk)
            w_out = mla_w_out_l
        next_is_sgu = i + 1 < depth and (i + 1) % 2 == 0
        xf, h = _out_proj(y, w_out, j, xf, post_g3, pre_g3, mod, i, seq, emit_h=next_is_sgu)
    return xf.reshape(batch, seq, d)
```

```python
import functools

import jax
import jax.numpy as jnp
from jax import lax
from jax.experimental import pallas as pl
from jax.experimental.pallas import tpu as pltpu

F32 = jnp.float32
BF16 = jnp.bfloat16

NORM_EPS = 1e-6
CHUNK = 64
SGU_BLOCK = 128
SGU_GROUP_DIM = 256
Q_LORA_RANK = 448
KV_LORA_RANK = 512
QK_NOPE_DIM = 128
QK_ROPE_DIM = 64
V_HEAD_DIM = 128
ROPE_THETA = 10000.0
LANES = 128
LATENT_PAD = 512
HEAD_PAD = 2 * LANES
ROPE_LANE0 = LANES - QK_ROPE_DIM
ATTN_KV_TILE = 256
ATTN_HEADS_PER_STEP = 2
ONES_ROWS = 16
VMEM_LIMIT_BYTES = 60 * 1024 * 1024
_NN_DIMS = (((1,), (0,)), ((), ()))
_NT_DIMS = (((1,), (1,)), ((), ()))
_TN_DIMS = (((0,), (0,)), ((), ()))


def _tile(n, pref, mult=8):
    t = min(n, pref)
    while t > mult and (n % t or t % mult):
        t -= mult
    return t if n % t == 0 else n


def _params(*semantics):
    return pltpu.CompilerParams(dimension_semantics=semantics, vmem_limit_bytes=VMEM_LIMIT_BYTES)


def _gelu(x):
    return 0.5 * x * (1.0 + lax.erf(x * (0.5 ** 0.5)))


def _silu(x):
    return x * jax.nn.sigmoid(x)


def _modulated_norm(x, g, shift, scale):
    ms = jnp.mean(x * x, axis=-1, keepdims=True)
    y = x * lax.rsqrt(ms + NORM_EPS) * g
    return y * (1.0 + scale) + shift


def _rope_rotate(x, cos_t, sin_t):
    lane = lax.broadcasted_iota(jnp.int32, x.shape, 1)
    half = QK_ROPE_DIM // 2
    partner = jnp.where(lane < ROPE_LANE0 + half,
                        pltpu.roll(x, LANES - half, 1),
                        pltpu.roll(x, half, 1))
    return x * cos_t + partner * sin_t


ROW_BLOCK = 256


def _pipelined(n_blocks, produce, consume):
    pending = None
    for r in range(n_blocks):
        current = produce(r)
        if pending is not None:
            consume(r - 1, pending)
        pending = current
    consume(n_blocks - 1, pending)


def _layer_spec(layer, rows, cols):
    return pl.BlockSpec((None, rows, cols), lambda *_: (layer, 0, 0), pipeline_mode=pl.Buffered(1))


def _mod_specs(layer, tiles_per_batch, d):
    def spec(k):
        return pl.BlockSpec((None, None, None, 1, d),
                            lambda m, *_: (layer, m // tiles_per_batch, k, 0, 0))
    return spec(0), spec(1), spec(2)


def _mod_kernel(c_ref, w_ref, b_ref, o_ref):
    cond = _silu(c_ref[...]).astype(BF16)
    o_ref[0] = jnp.dot(cond, w_ref[0].astype(BF16), preferred_element_type=F32) + b_ref[0]


def _adaln_mod(c, ada_w, ada_b):
    depth, d, n = ada_w.shape
    b = c.shape[0]
    tn = _tile(n, 1536, LANES)
    return pl.pallas_call(
        _mod_kernel,
        grid=(depth, n // tn),
        in_specs=[
            pl.BlockSpec((b, d), lambda i, j: (0, 0)),
            pl.BlockSpec((1, d, tn), lambda i, j: (i, 0, j)),
            pl.BlockSpec((1, 1, tn), lambda i, j: (i, 0, j)),
        ],
        out_specs=pl.BlockSpec((1, b, tn), lambda i, j: (i, 0, j)),
        out_shape=jax.ShapeDtypeStruct((depth, b, n), F32),
        compiler_params=_params("arbitrary", "arbitrary"),
        name="adaln_mod",
    )(c, ada_w, ada_b.reshape(depth, 1, n))


def _pre_norm_kernel(x_ref, g_ref, shift_ref, scale_ref, h_ref):
    h_ref[...] = _modulated_norm(x_ref[...], g_ref[...], shift_ref[...], scale_ref[...]).astype(BF16)


def _pre_norm(x, pre_g, mod, layer, seq):
    m, d = x.shape
    tm = _tile(seq, 1024)
    shift_spec, scale_spec, _ = _mod_specs(layer, seq // tm, d)
    return pl.pallas_call(
        _pre_norm_kernel,
        grid=(m // tm,),
        in_specs=[pl.BlockSpec((tm, d), lambda i: (i, 0)), _layer_spec(layer, 1, d),
                  shift_spec, scale_spec],
        out_specs=pl.BlockSpec((tm, d), lambda i: (i, 0)),
        out_shape=jax.ShapeDtypeStruct((m, d), BF16),
        compiler_params=_params("arbitrary"),
        name="pre_norm",
    )(x, pre_g, mod, mod)


def _sgu_v_kernel(h_ref, w_ref, vg_ref, mu_ref, rstd_ref, s1_ref, s2_ref, *, width):
    n = pl.program_id(1)

    @pl.when(n == 0)
    def _():
        s1_ref[...] = jnp.zeros_like(s1_ref)
        s2_ref[...] = jnp.zeros_like(s2_ref)

    tr = min(ROW_BLOCK, h_ref.shape[0])

    def project(r):
        return jnp.dot(h_ref[r * tr:(r + 1) * tr, :], w_ref[...], preferred_element_type=F32)

    def activate(r, acc):
        rows = slice(r * tr, (r + 1) * tr)
        v = _gelu(acc)
        vg_ref[rows, :] = v.astype(BF16)
        s1_ref[rows, :] += jnp.sum(v, axis=-1, keepdims=True)
        s2_ref[rows, :] += jnp.sum(v * v, axis=-1, keepdims=True)

    _pipelined(h_ref.shape[0] // tr, project, activate)

    @pl.when(n == pl.num_programs(1) - 1)
    def _():
        mu = s1_ref[...] * (1.0 / width)
        var = s2_ref[...] * (1.0 / width) - mu * mu
        mu_ref[...] = mu
        rstd_ref[...] = lax.rsqrt(var + NORM_EPS)


def _sgu_v(h, w_in, layer, seq):
    m, d = h.shape
    e = w_in.shape[2] // 3
    tm = _tile(seq, 1024, SGU_BLOCK)
    tn = _tile(e, 2048, LANES)
    return pl.pallas_call(
        functools.partial(_sgu_v_kernel, width=e),
        grid=(m // tm, e // tn),
        in_specs=[
            pl.BlockSpec((tm, d), lambda i, j: (i, 0)),
            pl.BlockSpec((None, d, tn), lambda i, j: (layer, 0, e // tn + j)),
        ],
        out_specs=[
            pl.BlockSpec((tm, tn), lambda i, j: (i, j)),
            pl.BlockSpec((tm, 1), lambda i, j: (i, 0)),
            pl.BlockSpec((tm, 1), lambda i, j: (i, 0)),
        ],
        out_shape=[
            jax.ShapeDtypeStruct((m, e), BF16),
            jax.ShapeDtypeStruct((m, 1), F32),
            jax.ShapeDtypeStruct((m, 1), F32),
        ],
        scratch_shapes=[pltpu.VMEM((tm, 1), F32), pltpu.VMEM((tm, 1), F32)],
        compiler_params=_params("arbitrary", "arbitrary"),
        name="sgu_v",
    )(h, w_in)


def _sgu_gate_kernel(h_ref, wu_ref, wz_ref, vg_ref, mu_ref, rstd_ref, ng_ref, ws_ref, bs_ref,
                     y_ref, *, groups_per_step):
    tm = h_ref.shape[0]
    tr = min(ROW_BLOCK, tm)
    nblk = tr // SGU_BLOCK
    t_chunk = lax.broadcasted_iota(jnp.int32, (SGU_BLOCK, SGU_BLOCK), 0) // CHUNK
    s_chunk = lax.broadcasted_iota(jnp.int32, (SGU_BLOCK, SGU_BLOCK), 1) // CHUNK
    causal = s_chunk <= t_chunk
    ws = [jnp.where(causal, ws_ref[gi], 0.0).astype(BF16) for gi in range(groups_per_step)]

    def project(r):
        rows = slice(r * tr, (r + 1) * tr)
        h = h_ref[rows, :]
        vn = ((vg_ref[rows, :].astype(F32) - mu_ref[rows, :]) * rstd_ref[rows, :]
              * ng_ref[...]).astype(BF16)
        vms = []
        for gi in range(groups_per_step):
            cols = slice(gi * SGU_GROUP_DIM, (gi + 1) * SGU_GROUP_DIM)
            vcat = jnp.concatenate(
                [vn[b * SGU_BLOCK:(b + 1) * SGU_BLOCK, cols] for b in range(nblk)], axis=1)
            vms.append(jnp.dot(ws[gi], vcat, preferred_element_type=F32))
        return (jnp.dot(h, wu_ref[...], preferred_element_type=F32),
                jnp.dot(h, wz_ref[...], preferred_element_type=F32), vms)

    def gate(r, projected):
        u_acc, z_acc, vms = projected
        u = _gelu(u_acc)
        z = _silu(z_acc)
        for gi in range(groups_per_step):
            cols = slice(gi * SGU_GROUP_DIM, (gi + 1) * SGU_GROUP_DIM)
            vm = vms[gi] + bs_ref[gi]
            for b in range(nblk):
                rows = slice(b * SGU_BLOCK, (b + 1) * SGU_BLOCK)
                vmb = vm[:, b * SGU_GROUP_DIM:(b + 1) * SGU_GROUP_DIM]
                y_ref[r * tr + b * SGU_BLOCK:r * tr + (b + 1) * SGU_BLOCK, cols] = (
                    u[rows, cols] * vmb * z[rows, cols]).astype(BF16)

    _pipelined(tm // tr, project, gate)


def _sgu_gate(h, w_in, vg, mu, rstd, norm_g, w_s, b_s, layer, seq):
    m, d = h.shape
    e = w_in.shape[2] // 3
    groups = e // SGU_GROUP_DIM
    gs = next(g for g in (4, 2, 1) if groups % g == 0)
    tn = gs * SGU_GROUP_DIM
    tm = _tile(seq, 1024, SGU_BLOCK)
    return pl.pallas_call(
        functools.partial(_sgu_gate_kernel, groups_per_step=gs),
        grid=(m // tm, groups // gs),
        in_specs=[
            pl.BlockSpec((tm, d), lambda i, j: (i, 0)),
            pl.BlockSpec((None, d, tn), lambda i, j: (layer, 0, j)),
            pl.BlockSpec((None, d, tn), lambda i, j: (layer, 0, 2 * (e // tn) + j)),
            pl.BlockSpec((tm, tn), lambda i, j: (i, j)),
            pl.BlockSpec((tm, 1), lambda i, j: (i, 0)),
            pl.BlockSpec((tm, 1), lambda i, j: (i, 0)),
            pl.BlockSpec((None, 1, tn), lambda i, j: (layer, 0, j)),
            pl.BlockSpec((None, gs, SGU_BLOCK, SGU_BLOCK), lambda i, j: (layer, j, 0, 0)),
            pl.BlockSpec((None, gs, SGU_BLOCK, 1), lambda i, j: (layer, j, 0, 0)),
        ],
        out_specs=pl.BlockSpec((tm, tn), lambda i, j: (i, j)),
        out_shape=jax.ShapeDtypeStruct((m, e), BF16),
        compiler_params=_params("arbitrary", "arbitrary"),
        name="sgu_gate",
    )(h, w_in, w_in, vg, mu, rstd, norm_g, w_s, b_s)


def _out_proj_kernel(y_ref, w_ref, x_ref, g_ref, gate_ref, *rest, y_transposed, emit_h):
    if emit_h:
        next_g_ref, next_shift_ref, next_scale_ref, o_ref, h_ref = rest
    else:
        (o_ref,) = rest
    tm = o_ref.shape[0]
    tr = min(ROW_BLOCK, tm)

    def project(r):
        if y_transposed:
            return lax.dot_general(y_ref[:, r * tr:(r + 1) * tr], w_ref[...], _TN_DIMS,
                                   preferred_element_type=F32)
        return jnp.dot(y_ref[r * tr:(r + 1) * tr, :], w_ref[...], preferred_element_type=F32)

    def residual(r, acc):
        rows = slice(r * tr, (r + 1) * tr)
        ms = jnp.mean(acc * acc, axis=-1, keepdims=True)
        yn = acc * lax.rsqrt(ms + NORM_EPS) * g_ref[...]
        x_new = x_ref[rows, :] + gate_ref[...] * yn
        o_ref[rows, :] = x_new
        if emit_h:
            h_ref[rows, :] = _modulated_norm(x_new, next_g_ref[...], next_shift_ref[...],
                                             next_scale_ref[...]).astype(BF16)

    _pipelined(tm // tr, project, residual)


def _out_proj(y, w_out, mixer_layer, x, post_g, pre_g, mod, layer, seq, emit_h):
    m, d = x.shape
    k = w_out.shape[1]
    if y.ndim == 2:
        tm = _tile(seq, 512)
        y_spec = pl.BlockSpec((tm, k), lambda i: (i, 0))
    else:
        tm = y.shape[4]
        nq = seq // tm

        def pair_major(i):
            j = i % nq
            lower = j < nq // 2
            return (i // nq, jnp.where(lower, j, nq - 1 - j), jnp.where(lower, 0, 1), 0, 0)

        y_spec = pl.BlockSpec((None, None, None, k, tm), pair_major)
    tpb = seq // tm
    row = pl.BlockSpec((tm, d), lambda i: (i, 0))
    _, _, gate_spec = _mod_specs(layer, tpb, d)
    in_specs = [y_spec, _layer_spec(mixer_layer, k, d), row, _layer_spec(layer, 1, d), gate_spec]
    args = [y, w_out, x, post_g, mod]
    out_specs = [row]
    out_shape = [jax.ShapeDtypeStruct((m, d), F32)]
    if emit_h:
        next_shift_spec, next_scale_spec, _ = _mod_specs(layer + 1, tpb, d)
        in_specs += [_layer_spec(layer + 1, 1, d), next_shift_spec, next_scale_spec]
        args += [pre_g, mod, mod]
        out_specs.append(row)
        out_shape.append(jax.ShapeDtypeStruct((m, d), BF16))
    outs = pl.pallas_call(
        functools.partial(_out_proj_kernel, y_transposed=y.ndim != 2, emit_h=emit_h),
        grid=(m // tm,),
        in_specs=in_specs,
        out_specs=out_specs,
        out_shape=out_shape,
        compiler_params=_params("arbitrary"),
        name="out_proj",
    )(*args)
    return (outs[0], outs[1]) if emit_h else (outs[0], None)


def _mla_in_kernel(x_ref, g_ref, shift_ref, scale_ref, w_ref, wzt_ref, gq_ref, gkv_ref, cos_ref,
                   sin_ref, cq_ref, ckv_ref, kr_ref, szt_ref):
    h = _modulated_norm(x_ref[...], g_ref[...], shift_ref[...], scale_ref[...]).astype(BF16)
    lat = jnp.dot(h, w_ref[...], preferred_element_type=F32)
    cq = lat[:, :LATENT_PAD]
    ckv = lat[:, LATENT_PAD:2 * LATENT_PAD]
    kr = lat[:, 2 * LATENT_PAD:]
    ms_q = jnp.sum(cq * cq, axis=-1, keepdims=True) * (1.0 / Q_LORA_RANK)
    cq_ref[...] = (cq * lax.rsqrt(ms_q + NORM_EPS) * gq_ref[...]).astype(BF16)
    ms_kv = jnp.mean(ckv * ckv, axis=-1, keepdims=True)
    ckv_ref[...] = (ckv * lax.rsqrt(ms_kv + NORM_EPS) * gkv_ref[...]).astype(BF16)
    kr_ref[...] = _rope_rotate(kr, cos_ref[...], sin_ref[...]).astype(BF16)
    zt = lax.dot_general(wzt_ref[...], h, _NT_DIMS, preferred_element_type=F32)
    szt_ref[...] = _silu(zt).astype(BF16)


def _mla_in(x, pre_g, mod, w_lat, wzt, gq, gkv, cos_t, sin_t, layer, mixer_layer, seq):
    m, d = x.shape
    nlat = w_lat.shape[2]
    width = wzt.shape[1]
    tm = _tile(seq, 512)
    tpb = seq // tm
    row = lambda n: pl.BlockSpec((tm, n), lambda i: (i, 0))
    table = pl.BlockSpec((tm, LANES), lambda i: (i % tpb, 0))
    shift_spec, scale_spec, _ = _mod_specs(layer, tpb, d)
    j = mixer_layer
    return pl.pallas_call(
        _mla_in_kernel,
        grid=(m // tm,),
        in_specs=[
            row(d), _layer_spec(layer, 1, d), shift_spec, scale_spec,
            _layer_spec(j, d, nlat), _layer_spec(j, width, d),
            _layer_spec(j, 1, LATENT_PAD), _layer_spec(j, 1, LATENT_PAD), table, table,
        ],
        out_specs=[row(LATENT_PAD), row(LATENT_PAD), row(LANES),
                   pl.BlockSpec((width, tm), lambda i: (0, i))],
        out_shape=[
            jax.ShapeDtypeStruct((m, LATENT_PAD), BF16),
            jax.ShapeDtypeStruct((m, LATENT_PAD), BF16),
            jax.ShapeDtypeStruct((m, LANES), BF16),
            jax.ShapeDtypeStruct((width, m), BF16),
        ],
        compiler_params=_params("arbitrary"),
        name="mla_in",
    )(x, pre_g, mod, mod, w_lat, wzt, gq, gkv, cos_t, sin_t)


def _mla_up_kernel(cq_ref, ckv_ref, kr_ref, wqt_ref, wk_ref, wvt_ref, cos_ref, sin_ref,
                   qt_ref, k_ref, vt_ref, *, heads, scale, tk):
    ckv = ckv_ref[...]
    qt = lax.dot_general(wqt_ref[...], cq_ref[...], _NT_DIMS, preferred_element_type=F32) * scale
    cos_t = cos_ref[...]
    sin_t = sin_ref[...]
    half = QK_ROPE_DIM // 2
    for hd in range(heads):
        r0 = hd * HEAD_PAD
        r1 = r0 + HEAD_PAD - QK_ROPE_DIM
        qt_ref[r0:r1, :] = qt[r0:r1].astype(BF16)
        x1 = qt[r1:r1 + half]
        x2 = qt[r1 + half:r1 + QK_ROPE_DIM]
        qt_ref[r1:r1 + half, :] = (x1 * cos_t - x2 * sin_t).astype(BF16)
        qt_ref[r1 + half:r1 + QK_ROPE_DIM, :] = (x1 * sin_t + x2 * cos_t).astype(BF16)
    kn = jnp.dot(ckv, wk_ref[...], preferred_element_type=F32)
    kr = kr_ref[...]
    for hd in range(heads):
        c0 = hd * HEAD_PAD
        k_ref[:, c0:c0 + LANES] = kn[:, hd * QK_NOPE_DIM:(hd + 1) * QK_NOPE_DIM].astype(BF16)
        k_ref[:, c0 + LANES:c0 + HEAD_PAD] = kr
    vt = lax.dot_general(wvt_ref[...], ckv, _NT_DIMS, preferred_element_type=F32)
    for cb in range(vt.shape[1] // tk):
        vt_ref[cb] = vt[:, cb * tk:(cb + 1) * tk].astype(BF16)


def _mla_up(cq, ckv, kr, wqt, wk, wvt, cos_tt, sin_tt, layer, seq, heads, tk):
    m = cq.shape[0]
    tm = _tile(seq, 512, tk)
    tpb = seq // tm
    row = lambda n: pl.BlockSpec((tm, n), lambda i: (i, 0))
    whole = lambda a: _layer_spec(layer, a.shape[1], a.shape[2])
    table = pl.BlockSpec((QK_ROPE_DIM // 2, tm), lambda i: (0, i % tpb))
    scale = float((QK_NOPE_DIM + QK_ROPE_DIM) ** -0.5 * 1.4426950408889634)
    return pl.pallas_call(
        functools.partial(_mla_up_kernel, heads=heads, scale=scale, tk=tk),
        grid=(m // tm,),
        in_specs=[row(LATENT_PAD), row(LATENT_PAD), row(LANES), whole(wqt), whole(wk), whole(wvt),
                  table, table],
        out_specs=[
            pl.BlockSpec((heads * HEAD_PAD, tm), lambda i: (0, i)),
            row(heads * HEAD_PAD),
            pl.BlockSpec((tm // tk, heads * V_HEAD_DIM, tk), lambda i: (i, 0, 0)),
        ],
        out_shape=[
            jax.ShapeDtypeStruct((heads * HEAD_PAD, m), BF16),
            jax.ShapeDtypeStruct((m, heads * HEAD_PAD), BF16),
            jax.ShapeDtypeStruct((m // tk, heads * V_HEAD_DIM, tk), BF16),
        ],
        compiler_params=_params("arbitrary"),
        name="mla_up",
    )(cq, ckv, kr, wqt, wk, wvt, cos_tt, sin_tt)


def _attn_kernel(qa_ref, qb_ref, k_ref, vt_ref, sza_ref, szb_ref, o_ref, m_sc, acc_sc,
                 *, tk, nq, heads):
    i = pl.program_id(2)
    tq = 2 * tk
    ones = jnp.ones((ONES_ROWS, tk), BF16)
    q_refs = (qa_ref, qb_ref)
    sz_refs = (sza_ref, szb_ref)

    def masked(s):
        visible = (lax.broadcasted_iota(jnp.int32, (tk, tk), 0) // CHUNK
                   <= lax.broadcasted_iota(jnp.int32, (tk, tk), 1) // CHUNK)
        return jnp.where(visible, s, -1e30)

    def score_stage(tile):
        hd, slot, t, lane0, mask = tile
        k = k_ref[t * tk:(t + 1) * tk, hd * HEAD_PAD:(hd + 1) * HEAD_PAD]
        qt = q_refs[slot][hd * HEAD_PAD:(hd + 1) * HEAD_PAD, lane0:]
        s = jnp.dot(k, qt, preferred_element_type=F32)
        if mask == "left":
            s = jnp.concatenate([masked(s[:, :tk]), s[:, tk:]], axis=1)
        elif mask == "all":
            s = masked(s)
        return s, jnp.max(s, axis=0, keepdims=True)

    def exp_stage(tile, s, s_max):
        hd, slot, _, lane0, mask = tile
        state = 2 * hd + slot
        if mask == "left":
            m_sc[state] = s_max
            return jnp.exp2(s - s_max).astype(BF16), None
        lanes = pl.ds(lane0, tq - lane0)
        m_prev = m_sc[state, :, lanes]
        m_new = jnp.maximum(m_prev, s_max)
        m_sc[state, :, lanes] = m_new
        return jnp.exp2(s - m_new).astype(BF16), jnp.exp2(m_prev - m_new)

    def value_stage(tile, p, alpha):
        hd, slot, t, lane0, _ = tile
        state = 2 * hd + slot
        v_aug = jnp.concatenate(
            [vt_ref[t, hd * V_HEAD_DIM:(hd + 1) * V_HEAD_DIM, :], ones], axis=0)
        pv = jnp.dot(v_aug, p, preferred_element_type=F32)
        if alpha is None:
            acc_sc[state] = pv
        else:
            lanes = pl.ds(lane0, tq - lane0)
            acc_sc[state, :, lanes] = alpha * acc_sc[state, :, lanes] + pv

    def finalize(hd, slot):
        acc = acc_sc[2 * hd + slot]
        rows = slice(hd * V_HEAD_DIM, (hd + 1) * V_HEAD_DIM)
        out_t = acc[:V_HEAD_DIM] * (1.0 / acc[V_HEAD_DIM:V_HEAD_DIM + 1])
        o_ref[slot, rows, :] = (out_t * sz_refs[slot][rows, :].astype(F32)).astype(BF16)

    def run_pair(c):
        tiles = []
        last = {}
        for hd in range(heads):
            for slot, blk in ((1, nq - 1 - c), (0, c)):
                tiles.append((hd, slot, 2 * blk, 0, "left"))
                tiles.append((hd, slot, 2 * blk + 1, tk, "all"))
                tiles.extend((hd, slot, t, 0, None) for t in range(2 * blk))
                last[len(tiles) - 1] = (hd, slot)
        scored = {}
        exped = {}
        for n in range(len(tiles) + 2):
            if n < len(tiles):
                scored[n] = score_stage(tiles[n])
            if 0 <= n - 1 < len(tiles):
                exped[n - 1] = exp_stage(tiles[n - 1], *scored.pop(n - 1))
            if 0 <= n - 2 < len(tiles):
                value_stage(tiles[n - 2], *exped.pop(n - 2))
                if n - 2 in last:
                    finalize(*last[n - 2])

    for c in range(nq // 2):
        pl.when(i == c)(functools.partial(run_pair, c))


def _attention(qt, k, vt, szt, batch, seq, heads, tk):
    tq = 2 * tk
    nq = seq // tq
    hps = ATTN_HEADS_PER_STEP if heads % ATTN_HEADS_PER_STEP == 0 else 1
    q_spec = lambda f: pl.BlockSpec((hps * HEAD_PAD, tq), lambda b, h, i: (h, b * nq + f(i)))
    sz_spec = lambda f: pl.BlockSpec((hps * V_HEAD_DIM, tq), lambda b, h, i: (h, b * nq + f(i)))
    first = lambda i: i
    second = lambda i: nq - 1 - i
    return pl.pallas_call(
        functools.partial(_attn_kernel, tk=tk, nq=nq, heads=hps),
        grid=(batch, heads // hps, nq // 2),
        in_specs=[
            q_spec(first), q_spec(second),
            pl.BlockSpec((seq, hps * HEAD_PAD), lambda b, h, i: (b, h)),
            pl.BlockSpec((seq // tk, hps * V_HEAD_DIM, tk), lambda b, h, i: (b, h, 0)),
            sz_spec(first), sz_spec(second),
        ],
        out_specs=pl.BlockSpec((None, None, 2, hps * V_HEAD_DIM, tq),
                               lambda b, h, i: (b, i, 0, h, 0)),
        out_shape=jax.ShapeDtypeStruct((batch, nq // 2, 2, heads * V_HEAD_DIM, tq), BF16),
        scratch_shapes=[
            pltpu.VMEM((2 * hps, 1, tq), F32),
            pltpu.VMEM((2 * hps, V_HEAD_DIM + ONES_ROWS, tq), F32),
        ],
        compiler_params=_params("arbitrary", "arbitrary", "arbitrary"),
        name="mla_attention",
    )(qt, qt, k, vt, szt, szt)


def _rope_tables(seq):
    pos = jnp.arange(seq, dtype=F32)
    inv_freq = ROPE_THETA ** (-jnp.arange(0, QK_ROPE_DIM, 2, dtype=F32) / QK_ROPE_DIM)
    ang = pos[:, None] * inv_freq[None, :]
    cos, sin = jnp.cos(ang), jnp.sin(ang)
    ones = jnp.ones((seq, ROPE_LANE0), F32)
    zeros = jnp.zeros((seq, ROPE_LANE0), F32)
    return (jnp.concatenate([ones, cos, cos], axis=-1),
            jnp.concatenate([zeros, -sin, sin], axis=-1), cos.T, sin.T)


def _mla_weight_layout(w_in, gq, w_uq, w_ukv, heads):
    nb, d, _ = w_in.shape
    o1, o2, o3 = Q_LORA_RANK, Q_LORA_RANK + KV_LORA_RANK, Q_LORA_RANK + KV_LORA_RANK + QK_ROPE_DIM
    zc = lambda n: jnp.zeros((nb, d, n), w_in.dtype)
    w_lat = jnp.concatenate([
        w_in[..., :o1], zc(LATENT_PAD - Q_LORA_RANK),
        w_in[..., o1:o2], zc(LATENT_PAD - KV_LORA_RANK),
        zc(ROPE_LANE0), w_in[..., o2:o3],
    ], axis=-1).astype(BF16)
    wzt = jnp.swapaxes(w_in[..., o3:], 1, 2).astype(BF16)
    gq_l = jnp.pad(gq, ((0, 0), (0, LATENT_PAD - Q_LORA_RANK)))[:, None, :]
    wq = w_uq.reshape(nb, Q_LORA_RANK, heads, QK_NOPE_DIM + QK_ROPE_DIM)
    wq = jnp.concatenate([
        wq[..., :QK_NOPE_DIM],
        jnp.zeros((nb, Q_LORA_RANK, heads, ROPE_LANE0), wq.dtype),
        wq[..., QK_NOPE_DIM:],
    ], axis=-1).reshape(nb, Q_LORA_RANK, heads * HEAD_PAD)
    wqt = jnp.swapaxes(
        jnp.pad(wq, ((0, 0), (0, LATENT_PAD - Q_LORA_RANK), (0, 0))), 1, 2).astype(BF16)
    wkv = w_ukv.reshape(nb, KV_LORA_RANK, heads, QK_NOPE_DIM + V_HEAD_DIM)
    wk = wkv[..., :QK_NOPE_DIM].reshape(nb, KV_LORA_RANK, heads * QK_NOPE_DIM).astype(BF16)
    wvt = jnp.swapaxes(
        wkv[..., QK_NOPE_DIM:].reshape(nb, KV_LORA_RANK, heads * V_HEAD_DIM), 1, 2).astype(BF16)
    return w_lat, wzt, gq_l, wqt, wk, wvt


def kernel(x, c, ada_w, ada_b, pre_g, post_g, sgu_w_in, sgu_norm_g, sgu_w_s, sgu_b_s, sgu_w_out,
           mla_w_in, mla_q_norm_g, mla_kv_norm_g, mla_w_uq, mla_w_ukv, mla_w_out):
    batch, seq, d = x.shape
    depth = ada_w.shape[0]
    heads = mla_w_out.shape[1] // V_HEAD_DIM
    tk = ATTN_KV_TILE
    assert seq % SGU_BLOCK == 0 and seq % (2 * tk) == 0 and d % LANES == 0
    assert sgu_w_in.shape[2] % (3 * SGU_GROUP_DIM) == 0
    assert mla_kv_norm_g.shape[1] == KV_LORA_RANK == LATENT_PAD

    mod = _adaln_mod(c, ada_w, ada_b).reshape(depth, batch, 3, 1, d)
    pre_g3 = pre_g[:, None, :]
    post_g3 = post_g[:, None, :]
    sgu_w_in_l = sgu_w_in.astype(BF16)
    sgu_w_out_l = sgu_w_out.astype(BF16)
    sgu_norm_g3 = sgu_norm_g[:, None, :]
    sgu_b_s4 = sgu_b_s[..., None]
    w_lat, wzt, gq_l, wqt, wk, wvt = _mla_weight_layout(
        mla_w_in, mla_q_norm_g, mla_w_uq, mla_w_ukv, heads)
    gkv_l = mla_kv_norm_g[:, None, :]
    mla_w_out_l = mla_w_out.astype(BF16)
    cos_t, sin_t, cos_tt, sin_tt = _rope_tables(seq)

    xf = x.reshape(batch * seq, d)
    h = _pre_norm(xf, pre_g3, mod, 0, seq)
    for i in range(depth):
        j = i // 2
        if i % 2 == 0:
            vg, mu, rstd = _sgu_v(h, sgu_w_in_l, j, seq)
            y = _sgu_gate(h, sgu_w_in_l, vg, mu, rstd, sgu_norm_g3, sgu_w_s, sgu_b_s4, j, seq)
            w_out = sgu_w_out_l
        else:
            cq, ckv, kr, szt = _mla_in(xf, pre_g3, mod, w_lat, wzt, gq_l, gkv_l, cos_t, sin_t,
                                       i, j, seq)
            qt, k, vt = _mla_up(cq, ckv, kr, wqt, wk, wvt, cos_tt, sin_tt, j, seq, heads, tk)
            y = _attention(qt, k, vt, szt, batch, seq, heads, tk)
            w_out = mla_w_out_l
        next_is_sgu = i + 1 < depth and (i + 1) % 2 == 0
        xf, h = _out_proj(y, w_out, j, xf, post_g3, pre_g3, mod, i, seq, emit_h=next_is_sgu)
    return xf.reshape(batch, seq, d)
```

```python
import functools

import jax
import jax.numpy as jnp
from jax import lax
from jax.experimental import pallas as pl
from jax.experimental.pallas import tpu as pltpu

F32 = jnp.float32
BF16 = jnp.bfloat16

NORM_EPS = 1e-6
CHUNK = 64
SGU_BLOCK = 128
SGU_GROUP_DIM = 256
Q_LORA_RANK = 448
KV_LORA_RANK = 512
QK_NOPE_DIM = 128
QK_ROPE_DIM = 64
V_HEAD_DIM = 128
ROPE_THETA = 10000.0
LANES = 128
LATENT_PAD = 512
HEAD_PAD = 2 * LANES
ROPE_LANE0 = LANES - QK_ROPE_DIM
ATTN_KV_TILE = 256
ATTN_HEADS_PER_STEP = 2
ONES_ROWS = 16
VMEM_LIMIT_BYTES = 60 * 1024 * 1024
_NN_DIMS = (((1,), (0,)), ((), ()))
_NT_DIMS = (((1,), (1,)), ((), ()))
_TN_DIMS = (((0,), (0,)), ((), ()))


def _tile(n, pref, mult=8):
    t = min(n, pref)
    while t > mult and (n % t or t % mult):
        t -= mult
    return t if n % t == 0 else n


def _params(*semantics):
    return pltpu.CompilerParams(dimension_semantics=semantics, vmem_limit_bytes=VMEM_LIMIT_BYTES)


def _gelu(x):
    return 0.5 * x * (1.0 + lax.erf(x * (0.5 ** 0.5)))


def _silu(x):
    return x * jax.nn.sigmoid(x)


def _modulated_norm(x, g, shift, scale):
    ms = jnp.mean(x * x, axis=-1, keepdims=True)
    y = x * lax.rsqrt(ms + NORM_EPS) * g
    return y * (1.0 + scale) + shift


def _rope_rotate(x, cos_t, sin_t):
    lane = lax.broadcasted_iota(jnp.int32, x.shape, 1)
    half = QK_ROPE_DIM // 2
    partner = jnp.where(lane < ROPE_LANE0 + half,
                        pltpu.roll(x, LANES - half, 1),
                        pltpu.roll(x, half, 1))
    return x * cos_t + partner * sin_t


ROW_BLOCK = 256


def _pipelined(n_blocks, produce, consume):
    pending = None
    for r in range(n_blocks):
        current = produce(r)
        if pending is not None:
            consume(r - 1, pending)
        pending = current
    consume(n_blocks - 1, pending)


def _layer_spec(layer, rows, cols):
    return pl.BlockSpec((None, rows, cols), lambda *_: (layer, 0, 0), pipeline_mode=pl.Buffered(1))


def _mod_specs(layer, tiles_per_batch, d):
    def spec(k):
        return pl.BlockSpec((None, None, None, 1, d),
                            lambda m, *_: (layer, m // tiles_per_batch, k, 0, 0))
    return spec(0), spec(1), spec(2)


def _mod_kernel(c_ref, w_ref, b_ref, o_ref):
    cond = _silu(c_ref[...]).astype(BF16)
    o_ref[0] = jnp.dot(cond, w_ref[0].astype(BF16), preferred_element_type=F32) + b_ref[0]


def _adaln_mod(c, ada_w, ada_b):
    depth, d, n = ada_w.shape
    b = c.shape[0]
    tn = _tile(n, 1536, LANES)
    return pl.pallas_call(
        _mod_kernel,
        grid=(depth, n // tn),
        in_specs=[
            pl.BlockSpec((b, d), lambda i, j: (0, 0)),
            pl.BlockSpec((1, d, tn), lambda i, j: (i, 0, j)),
            pl.BlockSpec((1, 1, tn), lambda i, j: (i, 0, j)),
        ],
        out_specs=pl.BlockSpec((1, b, tn), lambda i, j: (i, 0, j)),
        out_shape=jax.ShapeDtypeStruct((depth, b, n), F32),
        compiler_params=_params("arbitrary", "arbitrary"),
        name="adaln_mod",
    )(c, ada_w, ada_b.reshape(depth, 1, n))


def _sgu_v_kernel(*refs, width, fused_norm):
    if fused_norm:
        x_ref, g_ref, shift_ref, scale_ref, w_ref, vg_ref, mu_ref, rstd_ref, h_ref, s1_ref, s2_ref = refs
    else:
        h_ref, w_ref, vg_ref, mu_ref, rstd_ref, s1_ref, s2_ref = refs
    n = pl.program_id(1)

    @pl.when(n == 0)
    def _():
        s1_ref[...] = jnp.zeros_like(s1_ref)
        s2_ref[...] = jnp.zeros_like(s2_ref)

    tr = min(ROW_BLOCK, h_ref.shape[0])

    def project(r):
        rows = slice(r * tr, (r + 1) * tr)
        if fused_norm:
            h = _modulated_norm(x_ref[rows, :], g_ref[...], shift_ref[...],
                                scale_ref[...]).astype(BF16)
            h_ref[rows, :] = h
        else:
            h = h_ref[rows, :]
        return jnp.dot(h, w_ref[...], preferred_element_type=F32)

    def activate(r, acc):
        rows = slice(r * tr, (r + 1) * tr)
        v = _gelu(acc)
        vg_ref[rows, :] = v.astype(BF16)
        s1_ref[rows, :] += jnp.sum(v, axis=-1, keepdims=True)
        s2_ref[rows, :] += jnp.sum(v * v, axis=-1, keepdims=True)

    _pipelined(h_ref.shape[0] // tr, project, activate)

    @pl.when(n == pl.num_programs(1) - 1)
    def _():
        mu = s1_ref[...] * (1.0 / width)
        var = s2_ref[...] * (1.0 / width) - mu * mu
        mu_ref[...] = mu
        rstd_ref[...] = lax.rsqrt(var + NORM_EPS)


def _sgu_v(h, w_in, mixer_layer, seq, norm=None):
    fused_norm = norm is not None
    m, d = norm[0].shape if fused_norm else h.shape
    e = w_in.shape[2] // 3
    tm = _tile(seq, 512 if fused_norm else 1024, SGU_BLOCK)
    tn = _tile(e, 4096, LANES)
    assert tn == e or not fused_norm
    w_mode = dict(pipeline_mode=pl.Buffered(1)) if tn == e else {}
    row = pl.BlockSpec((tm, d), lambda i, j: (i, 0))
    stat = pl.BlockSpec((tm, 1), lambda i, j: (i, 0))
    w_spec = pl.BlockSpec((None, d, tn), lambda i, j: (mixer_layer, 0, e // tn + j), **w_mode)
    out_specs = [pl.BlockSpec((tm, tn), lambda i, j: (i, j)), stat, stat]
    out_shape = [jax.ShapeDtypeStruct((m, e), BF16), jax.ShapeDtypeStruct((m, 1), F32),
                 jax.ShapeDtypeStruct((m, 1), F32)]
    if fused_norm:
        x, pre_g, mod, layer = norm
        shift_spec, scale_spec, _ = _mod_specs(layer, seq // tm, d)
        in_specs = [row, _layer_spec(layer, 1, d), shift_spec, scale_spec, w_spec]
        args = (x, pre_g, mod, mod, w_in)
        out_specs.append(row)
        out_shape.append(jax.ShapeDtypeStruct((m, d), BF16))
    else:
        in_specs = [row, w_spec]
        args = (h, w_in)
    outs = pl.pallas_call(
        functools.partial(_sgu_v_kernel, width=e, fused_norm=fused_norm),
        grid=(m // tm, e // tn),
        in_specs=in_specs,
        out_specs=out_specs,
        out_shape=out_shape,
        scratch_shapes=[pltpu.VMEM((tm, 1), F32), pltpu.VMEM((tm, 1), F32)],
        compiler_params=_params("arbitrary", "arbitrary"),
        name="sgu_v",
    )(*args)
    return (*outs[:3], outs[3] if fused_norm else h)


def _sgu_gate_kernel(h_ref, wu_ref, wz_ref, vg_ref, mu_ref, rstd_ref, ng_ref, ws_ref, bs_ref,
                     y_ref, *, groups_per_step):
    tm = h_ref.shape[0]
    tr = min(ROW_BLOCK, tm)
    nblk = tr // SGU_BLOCK
    t_chunk = lax.broadcasted_iota(jnp.int32, (SGU_BLOCK, SGU_BLOCK), 0) // CHUNK
    s_chunk = lax.broadcasted_iota(jnp.int32, (SGU_BLOCK, SGU_BLOCK), 1) // CHUNK
    causal = s_chunk <= t_chunk
    ws = [jnp.where(causal, ws_ref[gi], 0.0).astype(BF16) for gi in range(groups_per_step)]

    def project(r):
        rows = slice(r * tr, (r + 1) * tr)
        h = h_ref[rows, :]
        vn = ((vg_ref[rows, :].astype(F32) - mu_ref[rows, :]) * rstd_ref[rows, :]
              * ng_ref[...]).astype(BF16)
        vms = []
        for gi in range(groups_per_step):
            cols = slice(gi * SGU_GROUP_DIM, (gi + 1) * SGU_GROUP_DIM)
            vcat = jnp.concatenate(
                [vn[b * SGU_BLOCK:(b + 1) * SGU_BLOCK, cols] for b in range(nblk)], axis=1)
            vms.append(jnp.dot(ws[gi], vcat, preferred_element_type=F32))
        return (jnp.dot(h, wu_ref[...], preferred_element_type=F32),
                jnp.dot(h, wz_ref[...], preferred_element_type=F32), vms)

    def gate(r, projected):
        u_acc, z_acc, vms = projected
        u = _gelu(u_acc)
        z = _silu(z_acc)
        for gi in range(groups_per_step):
            cols = slice(gi * SGU_GROUP_DIM, (gi + 1) * SGU_GROUP_DIM)
            vm = vms[gi] + bs_ref[gi]
            for b in range(nblk):
                rows = slice(b * SGU_BLOCK, (b + 1) * SGU_BLOCK)
                vmb = vm[:, b * SGU_GROUP_DIM:(b + 1) * SGU_GROUP_DIM]
                y_ref[r * tr + b * SGU_BLOCK:r * tr + (b + 1) * SGU_BLOCK, cols] = (
                    u[rows, cols] * vmb * z[rows, cols]).astype(BF16)

    _pipelined(tm // tr, project, gate)


def _sgu_gate(h, w_in, vg, mu, rstd, norm_g, w_s, b_s, layer, seq):
    m, d = h.shape
    e = w_in.shape[2] // 3
    groups = e // SGU_GROUP_DIM
    gs = next(g for g in (4, 2, 1) if groups % g == 0)
    tn = gs * SGU_GROUP_DIM
    tm = _tile(seq, 1024, SGU_BLOCK)
    return pl.pallas_call(
        functools.partial(_sgu_gate_kernel, groups_per_step=gs),
        grid=(m // tm, groups // gs),
        in_specs=[
            pl.BlockSpec((tm, d), lambda i, j: (i, 0)),
            pl.BlockSpec((None, d, tn), lambda i, j: (layer, 0, j)),
            pl.BlockSpec((None, d, tn), lambda i, j: (layer, 0, 2 * (e // tn) + j)),
            pl.BlockSpec((tm, tn), lambda i, j: (i, j)),
            pl.BlockSpec((tm, 1), lambda i, j: (i, 0)),
            pl.BlockSpec((tm, 1), lambda i, j: (i, 0)),
            pl.BlockSpec((None, 1, tn), lambda i, j: (layer, 0, j)),
            pl.BlockSpec((None, gs, SGU_BLOCK, SGU_BLOCK), lambda i, j: (layer, j, 0, 0)),
            pl.BlockSpec((None, gs, SGU_BLOCK, 1), lambda i, j: (layer, j, 0, 0)),
        ],
        out_specs=pl.BlockSpec((tm, tn), lambda i, j: (i, j)),
        out_shape=jax.ShapeDtypeStruct((m, e), BF16),
        compiler_params=_params("arbitrary", "arbitrary"),
        name="sgu_gate",
    )(h, w_in, w_in, vg, mu, rstd, norm_g, w_s, b_s)


def _out_proj_kernel(y_ref, w_ref, x_ref, g_ref, gate_ref, *rest, y_transposed, emit_h):
    if emit_h:
        next_g_ref, next_shift_ref, next_scale_ref, o_ref, h_ref = rest
    else:
        (o_ref,) = rest
    tm = o_ref.shape[0]
    tr = min(ROW_BLOCK, tm)

    def project(r):
        if y_transposed:
            return lax.dot_general(y_ref[:, r * tr:(r + 1) * tr], w_ref[...], _TN_DIMS,
                                   preferred_element_type=F32)
        return jnp.dot(y_ref[r * tr:(r + 1) * tr, :], w_ref[...], preferred_element_type=F32)

    def residual(r, acc):
        rows = slice(r * tr, (r + 1) * tr)
        ms = jnp.mean(acc * acc, axis=-1, keepdims=True)
        yn = acc * lax.rsqrt(ms + NORM_EPS) * g_ref[...]
        x_new = x_ref[rows, :] + gate_ref[...] * yn
        o_ref[rows, :] = x_new
        if emit_h:
            h_ref[rows, :] = _modulated_norm(x_new, next_g_ref[...], next_shift_ref[...],
                                             next_scale_ref[...]).astype(BF16)

    _pipelined(tm // tr, project, residual)


def _out_proj(y, w_out, mixer_layer, x, post_g, pre_g, mod, layer, seq, emit_h):
    m, d = x.shape
    k = w_out.shape[1]
    if y.ndim == 2:
        tm = _tile(seq, 512)
        y_spec = pl.BlockSpec((tm, k), lambda i: (i, 0))
    else:
        tm = y.shape[4]
        nq = seq // tm

        def pair_major(i):
            j = i % nq
            lower = j < nq // 2
            return (i // nq, jnp.where(lower, j, nq - 1 - j), jnp.where(lower, 0, 1), 0, 0)

        y_spec = pl.BlockSpec((None, None, None, k, tm), pair_major)
    tpb = seq // tm
    row = pl.BlockSpec((tm, d), lambda i: (i, 0))
    _, _, gate_spec = _mod_specs(layer, tpb, d)
    in_specs = [y_spec, _layer_spec(mixer_layer, k, d), row, _layer_spec(layer, 1, d), gate_spec]
    args = [y, w_out, x, post_g, mod]
    out_specs = [row]
    out_shape = [jax.ShapeDtypeStruct((m, d), F32)]
    if emit_h:
        next_shift_spec, next_scale_spec, _ = _mod_specs(layer + 1, tpb, d)
        in_specs += [_layer_spec(layer + 1, 1, d), next_shift_spec, next_scale_spec]
        args += [pre_g, mod, mod]
        out_specs.append(row)
        out_shape.append(jax.ShapeDtypeStruct((m, d), BF16))
    outs = pl.pallas_call(
        functools.partial(_out_proj_kernel, y_transposed=y.ndim != 2, emit_h=emit_h),
        grid=(m // tm,),
        in_specs=in_specs,
        out_specs=out_specs,
        out_shape=out_shape,
        compiler_params=_params("arbitrary"),
        name="out_proj",
    )(*args)
    return (outs[0], outs[1]) if emit_h else (outs[0], None)


def _mla_in_kernel(x_ref, g_ref, shift_ref, scale_ref, w_ref, wzt_ref, gq_ref, gkv_ref, cos_ref,
                   sin_ref, cq_ref, ckv_ref, kr_ref, szt_ref):
    h = _modulated_norm(x_ref[...], g_ref[...], shift_ref[...], scale_ref[...]).astype(BF16)
    lat = jnp.dot(h, w_ref[...], preferred_element_type=F32)
    cq = lat[:, :LATENT_PAD]
    ckv = lat[:, LATENT_PAD:2 * LATENT_PAD]
    kr = lat[:, 2 * LATENT_PAD:]
    ms_q = jnp.sum(cq * cq, axis=-1, keepdims=True) * (1.0 / Q_LORA_RANK)
    cq_ref[...] = (cq * lax.rsqrt(ms_q + NORM_EPS) * gq_ref[...]).astype(BF16)
    ms_kv = jnp.mean(ckv * ckv, axis=-1, keepdims=True)
    ckv_ref[...] = (ckv * lax.rsqrt(ms_kv + NORM_EPS) * gkv_ref[...]).astype(BF16)
    kr_ref[...] = _rope_rotate(kr, cos_ref[...], sin_ref[...]).astype(BF16)
    zt = lax.dot_general(wzt_ref[...], h, _NT_DIMS, preferred_element_type=F32)
    szt_ref[...] = _silu(zt).astype(BF16)


def _mla_in(x, pre_g, mod, w_lat, wzt, gq, gkv, cos_t, sin_t, layer, mixer_layer, seq):
    m, d = x.shape
    nlat = w_lat.shape[2]
    width = wzt.shape[1]
    tm = _tile(seq, 512)
    tpb = seq // tm
    row = lambda n: pl.BlockSpec((tm, n), lambda i: (i, 0))
    table = pl.BlockSpec((tm, LANES), lambda i: (i % tpb, 0))
    shift_spec, scale_spec, _ = _mod_specs(layer, tpb, d)
    j = mixer_layer
    return pl.pallas_call(
        _mla_in_kernel,
        grid=(m // tm,),
        in_specs=[
            row(d), _layer_spec(layer, 1, d), shift_spec, scale_spec,
            _layer_spec(j, d, nlat), _layer_spec(j, width, d),
            _layer_spec(j, 1, LATENT_PAD), _layer_spec(j, 1, LATENT_PAD), table, table,
        ],
        out_specs=[row(LATENT_PAD), row(LATENT_PAD), row(LANES),
                   pl.BlockSpec((width, tm), lambda i: (0, i))],
        out_shape=[
            jax.ShapeDtypeStruct((m, LATENT_PAD), BF16),
            jax.ShapeDtypeStruct((m, LATENT_PAD), BF16),
            jax.ShapeDtypeStruct((m, LANES), BF16),
            jax.ShapeDtypeStruct((width, m), BF16),
        ],
        compiler_params=_params("arbitrary"),
        name="mla_in",
    )(x, pre_g, mod, mod, w_lat, wzt, gq, gkv, cos_t, sin_t)


def _mla_up_kernel(cq_ref, ckv_ref, kr_ref, wqt_ref, wk_ref, wvt_ref, cos_ref, sin_ref,
                   qt_ref, k_ref, vt_ref, *, heads, scale, tk):
    ckv = ckv_ref[...]
    qt = lax.dot_general(wqt_ref[...], cq_ref[...], _NT_DIMS, preferred_element_type=F32) * scale
    cos_t = cos_ref[...]
    sin_t = sin_ref[...]
    half = QK_ROPE_DIM // 2
    for hd in range(heads):
        r0 = hd * HEAD_PAD
        r1 = r0 + HEAD_PAD - QK_ROPE_DIM
        qt_ref[r0:r1, :] = qt[r0:r1].astype(BF16)
        x1 = qt[r1:r1 + half]
        x2 = qt[r1 + half:r1 + QK_ROPE_DIM]
        qt_ref[r1:r1 + half, :] = (x1 * cos_t - x2 * sin_t).astype(BF16)
        qt_ref[r1 + half:r1 + QK_ROPE_DIM, :] = (x1 * sin_t + x2 * cos_t).astype(BF16)
    kn = jnp.dot(ckv, wk_ref[...], preferred_element_type=F32)
    kr = kr_ref[...]
    for hd in range(heads):
        c0 = hd * HEAD_PAD
        k_ref[:, c0:c0 + LANES] = kn[:, hd * QK_NOPE_DIM:(hd + 1) * QK_NOPE_DIM].astype(BF16)
        k_ref[:, c0 + LANES:c0 + HEAD_PAD] = kr
    vt = lax.dot_general(wvt_ref[...], ckv, _NT_DIMS, preferred_element_type=F32)
    for cb in range(vt.shape[1] // tk):
        vt_ref[cb] = vt[:, cb * tk:(cb + 1) * tk].astype(BF16)


def _mla_up(cq, ckv, kr, wqt, wk, wvt, cos_tt, sin_tt, layer, seq, heads, tk):
    m = cq.shape[0]
    tm = _tile(seq, 512, tk)
    tpb = seq // tm
    row = lambda n: pl.BlockSpec((tm, n), lambda i: (i, 0))
    whole = lambda a: _layer_spec(layer, a.shape[1], a.shape[2])
    table = pl.BlockSpec((QK_ROPE_DIM // 2, tm), lambda i: (0, i % tpb))
    scale = float((QK_NOPE_DIM + QK_ROPE_DIM) ** -0.5 * 1.4426950408889634)
    return pl.pallas_call(
        functools.partial(_mla_up_kernel, heads=heads, scale=scale, tk=tk),
        grid=(m // tm,),
        in_specs=[row(LATENT_PAD), row(LATENT_PAD), row(LANES), whole(wqt), whole(wk), whole(wvt),
                  table, table],
        out_specs=[
            pl.BlockSpec((heads * HEAD_PAD, tm), lambda i: (0, i)),
            row(heads * HEAD_PAD),
            pl.BlockSpec((tm // tk, heads * V_HEAD_DIM, tk), lambda i: (i, 0, 0)),
        ],
        out_shape=[
            jax.ShapeDtypeStruct((heads * HEAD_PAD, m), BF16),
            jax.ShapeDtypeStruct((m, heads * HEAD_PAD), BF16),
            jax.ShapeDtypeStruct((m // tk, heads * V_HEAD_DIM, tk), BF16),
        ],
        compiler_params=_params("arbitrary"),
        name="mla_up",
    )(cq, ckv, kr, wqt, wk, wvt, cos_tt, sin_tt)


def _attn_kernel(qa_ref, qb_ref, k_ref, vt_ref, sza_ref, szb_ref, o_ref, m_sc, acc_sc,
                 *, tk, nq, heads):
    i = pl.program_id(2)
    tq = 2 * tk
    q_refs = (qa_ref, qb_ref)
    sz_refs = (sza_ref, szb_ref)

    def masked(s):
        visible = (lax.broadcasted_iota(jnp.int32, (tk, tk), 0) // CHUNK
                   <= lax.broadcasted_iota(jnp.int32, (tk, tk), 1) // CHUNK)
        return jnp.where(visible, s, -1e30)

    def score_stage(tile):
        hd, slot, t, nt, lane0, mask = tile
        k = k_ref[t * tk:(t + nt) * tk, hd * HEAD_PAD:(hd + 1) * HEAD_PAD]
        qt = q_refs[slot][hd * HEAD_PAD:(hd + 1) * HEAD_PAD, lane0:]
        s = jnp.dot(k, qt, preferred_element_type=F32)
        if mask == "left":
            s = jnp.concatenate([masked(s[:, :tk]), s[:, tk:]], axis=1)
        elif mask == "all":
            s = masked(s)
        return s, jnp.max(s, axis=0, keepdims=True)

    def exp_stage(tile, s, s_max):
        hd, slot, _, _, lane0, mask = tile
        state = 2 * hd + slot
        if mask == "left":
            m_sc[state] = s_max
            return jnp.exp2(s - s_max).astype(BF16), None
        lanes = pl.ds(lane0, tq - lane0)
        m_prev = m_sc[state, :, lanes]
        m_new = jnp.maximum(m_prev, s_max)
        m_sc[state, :, lanes] = m_new
        return jnp.exp2(s - m_new).astype(BF16), jnp.exp2(m_prev - m_new)

    def value_stage(tile, p, alpha):
        hd, slot, t, nt, lane0, _ = tile
        state = 2 * hd + slot
        rows = slice(hd * V_HEAD_DIM, (hd + 1) * V_HEAD_DIM)
        v_t = jnp.concatenate([vt_ref[t + u, rows, :] for u in range(nt)], axis=1)
        v_aug = jnp.concatenate([v_t, jnp.ones((ONES_ROWS, nt * tk), BF16)], axis=0)
        pv = jnp.dot(v_aug, p, preferred_element_type=F32)
        if alpha is None:
            acc_sc[state] = pv
        else:
            lanes = pl.ds(lane0, tq - lane0)
            acc_sc[state, :, lanes] = alpha * acc_sc[state, :, lanes] + pv

    def finalize(hd, slot):
        acc = acc_sc[2 * hd + slot]
        rows = slice(hd * V_HEAD_DIM, (hd + 1) * V_HEAD_DIM)
        out_t = acc[:V_HEAD_DIM] * (1.0 / acc[V_HEAD_DIM:V_HEAD_DIM + 1])
        o_ref[slot, rows, :] = (out_t * sz_refs[slot][rows, :].astype(F32)).astype(BF16)

    def run_pair(c):
        tiles = []
        last = {}
        for hd in range(heads):
            for slot, blk in ((1, nq - 1 - c), (0, c)):
                tiles.append((hd, slot, 2 * blk, 1, 0, "left"))
                tiles.append((hd, slot, 2 * blk + 1, 1, tk, "all"))
                tiles.extend((hd, slot, t, 1, 0, None) for t in range(2 * blk))
                last[len(tiles) - 1] = (hd, slot)
        scored = {}
        exped = {}
        for n in range(len(tiles) + 2):
            if n < len(tiles):
                scored[n] = score_stage(tiles[n])
            if 0 <= n - 1 < len(tiles):
                exped[n - 1] = exp_stage(tiles[n - 1], *scored.pop(n - 1))
            if 0 <= n - 2 < len(tiles):
                value_stage(tiles[n - 2], *exped.pop(n - 2))
                if n - 2 in last:
                    finalize(*last[n - 2])

    for c in range(nq // 2):
        pl.when(i == c)(functools.partial(run_pair, c))


def _attention(qt, k, vt, szt, batch, seq, heads, tk):
    tq = 2 * tk
    nq = seq // tq
    hps = ATTN_HEADS_PER_STEP if heads % ATTN_HEADS_PER_STEP == 0 else 1
    q_spec = lambda f: pl.BlockSpec((hps * HEAD_PAD, tq), lambda b, h, i: (h, b * nq + f(i)))
    sz_spec = lambda f: pl.BlockSpec((hps * V_HEAD_DIM, tq), lambda b, h, i: (h, b * nq + f(i)))
    first = lambda i: i
    second = lambda i: nq - 1 - i
    return pl.pallas_call(
        functools.partial(_attn_kernel, tk=tk, nq=nq, heads=hps),
        grid=(batch, heads // hps, nq // 2),
        in_specs=[
            q_spec(first), q_spec(second),
            pl.BlockSpec((seq, hps * HEAD_PAD), lambda b, h, i: (b, h)),
            pl.BlockSpec((seq // tk, hps * V_HEAD_DIM, tk), lambda b, h, i: (b, h, 0)),
            sz_spec(first), sz_spec(second),
        ],
        out_specs=pl.BlockSpec((None, None, 2, hps * V_HEAD_DIM, tq),
                               lambda b, h, i: (b, i, 0, h, 0)),
        out_shape=jax.ShapeDtypeStruct((batch, nq // 2, 2, heads * V_HEAD_DIM, tq), BF16),
        scratch_shapes=[
            pltpu.VMEM((2 * hps, 1, tq), F32),
            pltpu.VMEM((2 * hps, V_HEAD_DIM + ONES_ROWS, tq), F32),
        ],
        compiler_params=_params("arbitrary", "arbitrary", "arbitrary"),
        name="mla_attention",
    )(qt, qt, k, vt, szt, szt)


def _rope_tables(seq):
    pos = jnp.arange(seq, dtype=F32)
    inv_freq = ROPE_THETA ** (-jnp.arange(0, QK_ROPE_DIM, 2, dtype=F32) / QK_ROPE_DIM)
    ang = pos[:, None] * inv_freq[None, :]
    cos, sin = jnp.cos(ang), jnp.sin(ang)
    ones = jnp.ones((seq, ROPE_LANE0), F32)
    zeros = jnp.zeros((seq, ROPE_LANE0), F32)
    return (jnp.concatenate([ones, cos, cos], axis=-1),
            jnp.concatenate([zeros, -sin, sin], axis=-1), cos.T, sin.T)


def _mla_weight_layout(w_in, gq, w_uq, w_ukv, heads):
    nb, d, _ = w_in.shape
    o1, o2, o3 = Q_LORA_RANK, Q_LORA_RANK + KV_LORA_RANK, Q_LORA_RANK + KV_LORA_RANK + QK_ROPE_DIM
    zc = lambda n: jnp.zeros((nb, d, n), w_in.dtype)
    w_lat = jnp.concatenate([
        w_in[..., :o1], zc(LATENT_PAD - Q_LORA_RANK),
        w_in[..., o1:o2], zc(LATENT_PAD - KV_LORA_RANK),
        zc(ROPE_LANE0), w_in[..., o2:o3],
    ], axis=-1).astype(BF16)
    wzt = jnp.swapaxes(w_in[..., o3:], 1, 2).astype(BF16)
    gq_l = jnp.pad(gq, ((0, 0), (0, LATENT_PAD - Q_LORA_RANK)))[:, None, :]
    wq = w_uq.reshape(nb, Q_LORA_RANK, heads, QK_NOPE_DIM + QK_ROPE_DIM)
    wq = jnp.concatenate([
        wq[..., :QK_NOPE_DIM],
        jnp.zeros((nb, Q_LORA_RANK, heads, ROPE_LANE0), wq.dtype),
        wq[..., QK_NOPE_DIM:],
    ], axis=-1).reshape(nb, Q_LORA_RANK, heads * HEAD_PAD)
    wqt = jnp.swapaxes(
        jnp.pad(wq, ((0, 0), (0, LATENT_PAD - Q_LORA_RANK), (0, 0))), 1, 2).astype(BF16)
    wkv = w_ukv.reshape(nb, KV_LORA_RANK, heads, QK_NOPE_DIM + V_HEAD_DIM)
    wk = wkv[..., :QK_NOPE_DIM].reshape(nb, KV_LORA_RANK, heads * QK_NOPE_DIM).astype(BF16)
    wvt = jnp.swapaxes(
        wkv[..., QK_NOPE_DIM:].reshape(nb, KV_LORA_RANK, heads * V_HEAD_DIM), 1, 2).astype(BF16)
    return w_lat, wzt, gq_l, wqt, wk, wvt


def kernel(x, c, ada_w, ada_b, pre_g, post_g, sgu_w_in, sgu_norm_g, sgu_w_s, sgu_b_s, sgu_w_out,
           mla_w_in, mla_q_norm_g, mla_kv_norm_g, mla_w_uq, mla_w_ukv, mla_w_out):
    batch, seq, d = x.shape
    depth = ada_w.shape[0]
    heads = mla_w_out.shape[1] // V_HEAD_DIM
    tk = ATTN_KV_TILE
    assert seq % SGU_BLOCK == 0 and seq % (2 * tk) == 0 and d % LANES == 0
    assert sgu_w_in.shape[2] % (3 * SGU_GROUP_DIM) == 0
    assert mla_kv_norm_g.shape[1] == KV_LORA_RANK == LATENT_PAD

    mod = _adaln_mod(c, ada_w, ada_b).reshape(depth, batch, 3, 1, d)
    pre_g3 = pre_g[:, None, :]
    post_g3 = post_g[:, None, :]
    sgu_w_in_l = sgu_w_in.astype(BF16)
    sgu_w_out_l = sgu_w_out.astype(BF16)
    sgu_norm_g3 = sgu_norm_g[:, None, :]
    sgu_b_s4 = sgu_b_s[..., None]
    w_lat, wzt, gq_l, wqt, wk, wvt = _mla_weight_layout(
        mla_w_in, mla_q_norm_g, mla_w_uq, mla_w_ukv, heads)
    gkv_l = mla_kv_norm_g[:, None, :]
    mla_w_out_l = mla_w_out.astype(BF16)
    cos_t, sin_t, cos_tt, sin_tt = _rope_tables(seq)

    xf = x.reshape(batch * seq, d)
    h = None
    for i in range(depth):
        j = i // 2
        if i % 2 == 0:
            norm = (xf, pre_g3, mod, i) if h is None else None
            vg, mu, rstd, h = _sgu_v(h, sgu_w_in_l, j, seq, norm)
            y = _sgu_gate(h, sgu_w_in_l, vg, mu, rstd, sgu_norm_g3, sgu_w_s, sgu_b_s4, j, seq)
            w_out = sgu_w_out_l
        else:
            cq, ckv, kr, szt = _mla_in(xf, pre_g3, mod, w_lat, wzt, gq_l, gkv_l, cos_t, sin_t,
                                       i, j, seq)
            qt, k, vt = _mla_up(cq, ckv, kr, wqt, wk, wvt, cos_tt, sin_tt, j, seq, heads, tk)
            y = _attention(qt, k, vt, szt, batch, seq, heads, tk)
            w_out = mla_w_out_l
        next_is_sgu = i + 1 < depth and (i + 1) % 2 == 0
        xf, h = _out_proj(y, w_out, j, xf, post_g3, pre_g3, mod, i, seq, emit_h=next_is_sgu)
    return xf.reshape(batch, seq, d)
```

```python
import functools

import jax
import jax.numpy as jnp
from jax import lax
from jax.experimental import pallas as pl
from jax.experimental.pallas import tpu as pltpu

F32 = jnp.float32
BF16 = jnp.bfloat16

NORM_EPS = 1e-6
CHUNK = 64
SGU_BLOCK = 128
SGU_GROUP_DIM = 256
Q_LORA_RANK = 448
KV_LORA_RANK = 512
QK_NOPE_DIM = 128
QK_ROPE_DIM = 64
V_HEAD_DIM = 128
ROPE_THETA = 10000.0
LANES = 128
LATENT_PAD = 512
HEAD_PAD = 2 * LANES
ROPE_LANE0 = LANES - QK_ROPE_DIM
ATTN_KV_TILE = 256
ATTN_HEADS_PER_STEP = 4
ONES_ROWS = 16
VMEM_LIMIT_BYTES = 60 * 1024 * 1024
_NN_DIMS = (((1,), (0,)), ((), ()))
_NT_DIMS = (((1,), (1,)), ((), ()))
_TN_DIMS = (((0,), (0,)), ((), ()))


def _tile(n, pref, mult=8):
    t = min(n, pref)
    while t > mult and (n % t or t % mult):
        t -= mult
    return t if n % t == 0 else n


def _params(*semantics):
    return pltpu.CompilerParams(dimension_semantics=semantics, vmem_limit_bytes=VMEM_LIMIT_BYTES)


def _gelu(x):
    return 0.5 * x * (1.0 + lax.erf(x * (0.5 ** 0.5)))


def _silu(x):
    return x * jax.nn.sigmoid(x)


def _modulated_norm(x, g, shift, scale):
    ms = jnp.mean(x * x, axis=-1, keepdims=True)
    y = x * lax.rsqrt(ms + NORM_EPS) * g
    return y * (1.0 + scale) + shift


def _rope_rotate(x, cos_t, sin_t):
    lane = lax.broadcasted_iota(jnp.int32, x.shape, 1)
    half = QK_ROPE_DIM // 2
    partner = jnp.where(lane < ROPE_LANE0 + half,
                        pltpu.roll(x, LANES - half, 1),
                        pltpu.roll(x, half, 1))
    return x * cos_t + partner * sin_t


ROW_BLOCK = 256


def _pipelined(n_blocks, produce, consume):
    pending = None
    for r in range(n_blocks):
        current = produce(r)
        if pending is not None:
            consume(r - 1, pending)
        pending = current
    consume(n_blocks - 1, pending)


def _layer_spec(layer, rows, cols):
    return pl.BlockSpec((None, rows, cols), lambda *_: (layer, 0, 0), pipeline_mode=pl.Buffered(1))


def _mod_specs(layer, tiles_per_batch, d):
    def spec(k):
        return pl.BlockSpec((None, None, None, 1, d),
                            lambda m, *_: (layer, m // tiles_per_batch, k, 0, 0))
    return spec(0), spec(1), spec(2)


def _mod_kernel(c_ref, w_ref, b_ref, o_ref):
    cond = _silu(c_ref[...]).astype(BF16)
    o_ref[0] = jnp.dot(cond, w_ref[0].astype(BF16), preferred_element_type=F32) + b_ref[0]


def _adaln_mod(c, ada_w, ada_b):
    depth, d, n = ada_w.shape
    b = c.shape[0]
    tn = _tile(n, 1536, LANES)
    return pl.pallas_call(
        _mod_kernel,
        grid=(depth, n // tn),
        in_specs=[
            pl.BlockSpec((b, d), lambda i, j: (0, 0)),
            pl.BlockSpec((1, d, tn), lambda i, j: (i, 0, j)),
            pl.BlockSpec((1, 1, tn), lambda i, j: (i, 0, j)),
        ],
        out_specs=pl.BlockSpec((1, b, tn), lambda i, j: (i, 0, j)),
        out_shape=jax.ShapeDtypeStruct((depth, b, n), F32),
        compiler_params=_params("arbitrary", "arbitrary"),
        name="adaln_mod",
    )(c, ada_w, ada_b.reshape(depth, 1, n))


def _sgu_v_kernel(*refs, width, fused_norm):
    if fused_norm:
        x_ref, g_ref, shift_ref, scale_ref, w_ref, vg_ref, mu_ref, rstd_ref, h_ref, s1_ref, s2_ref = refs
    else:
        h_ref, w_ref, vg_ref, mu_ref, rstd_ref, s1_ref, s2_ref = refs
    n = pl.program_id(1)

    @pl.when(n == 0)
    def _():
        s1_ref[...] = jnp.zeros_like(s1_ref)
        s2_ref[...] = jnp.zeros_like(s2_ref)

    tm = vg_ref.shape[0]
    tr = min(ROW_BLOCK, tm)

    def project(r):
        rows = slice(r * tr, (r + 1) * tr)
        if fused_norm:
            h = _modulated_norm(x_ref[rows, :], g_ref[...], shift_ref[...],
                                scale_ref[...]).astype(BF16)
            h_ref[rows, :] = h
        else:
            h = h_ref[rows, :]
        return jnp.dot(h, w_ref[...], preferred_element_type=F32)

    def activate(r, acc):
        rows = slice(r * tr, (r + 1) * tr)
        v = _gelu(acc)
        vg_ref[rows, :] = v.astype(BF16)
        s1_ref[rows, :] += jnp.sum(v, axis=-1, keepdims=True)
        s2_ref[rows, :] += jnp.sum(v * v, axis=-1, keepdims=True)

    _pipelined(tm // tr, project, activate)

    @pl.when(n == pl.num_programs(1) - 1)
    def _():
        mu = s1_ref[...] * (1.0 / width)
        var = s2_ref[...] * (1.0 / width) - mu * mu
        mu_ref[...] = mu
        rstd_ref[...] = lax.rsqrt(var + NORM_EPS)


def _sgu_v(h, w_in, mixer_layer, seq, norm=None):
    fused_norm = norm is not None
    m, d = norm[0].shape if fused_norm else h.shape
    e = w_in.shape[2] // 3
    tm = _tile(seq, 512 if fused_norm else 1024, SGU_BLOCK)
    tn = _tile(e, 4096, LANES)
    assert tn == e or not fused_norm
    w_mode = dict(pipeline_mode=pl.Buffered(1)) if tn == e else {}
    row = pl.BlockSpec((tm, d), lambda i, j: (i, 0))
    stat = pl.BlockSpec((tm, 1), lambda i, j: (i, 0))
    w_spec = pl.BlockSpec((None, d, tn), lambda i, j: (mixer_layer, 0, e // tn + j), **w_mode)
    out_specs = [pl.BlockSpec((tm, tn), lambda i, j: (i, j)), stat, stat]
    out_shape = [jax.ShapeDtypeStruct((m, e), BF16), jax.ShapeDtypeStruct((m, 1), F32),
                 jax.ShapeDtypeStruct((m, 1), F32)]
    if fused_norm:
        x, pre_g, mod, layer = norm
        shift_spec, scale_spec, _ = _mod_specs(layer, seq // tm, d)
        in_specs = [row, _layer_spec(layer, 1, d), shift_spec, scale_spec, w_spec]
        args = (x, pre_g, mod, mod, w_in)
        out_specs.append(row)
        out_shape.append(jax.ShapeDtypeStruct((m, d), BF16))
    else:
        in_specs = [row, w_spec]
        args = (h, w_in)
    outs = pl.pallas_call(
        functools.partial(_sgu_v_kernel, width=e, fused_norm=fused_norm),
        grid=(m // tm, e // tn),
        in_specs=in_specs,
        out_specs=out_specs,
        out_shape=out_shape,
        scratch_shapes=[pltpu.VMEM((tm, 1), F32), pltpu.VMEM((tm, 1), F32)],
        compiler_params=_params("arbitrary", "arbitrary"),
        name="sgu_v",
    )(*args)
    return (*outs[:3], outs[3] if fused_norm else h)


def _sgu_gate_kernel(h_ref, wu_ref, wz_ref, vg_ref, mu_ref, rstd_ref, ng_ref, ws_ref, bs_ref,
                     y_ref, *, groups_per_step):
    tm = y_ref.shape[0]
    tr = min(ROW_BLOCK, tm)
    nblk = tr // SGU_BLOCK
    t_chunk = lax.broadcasted_iota(jnp.int32, (SGU_BLOCK, SGU_BLOCK), 0) // CHUNK
    s_chunk = lax.broadcasted_iota(jnp.int32, (SGU_BLOCK, SGU_BLOCK), 1) // CHUNK
    causal = s_chunk <= t_chunk
    ws = [jnp.where(causal, ws_ref[gi], 0.0).astype(BF16) for gi in range(groups_per_step)]

    def project(r):
        rows = slice(r * tr, (r + 1) * tr)
        h = h_ref[rows, :]
        vn = ((vg_ref[rows, :].astype(F32) - mu_ref[rows, :]) * rstd_ref[rows, :]
              * ng_ref[...]).astype(BF16)
        vms = []
        for gi in range(groups_per_step):
            cols = slice(gi * SGU_GROUP_DIM, (gi + 1) * SGU_GROUP_DIM)
            vcat = jnp.concatenate(
                [vn[b * SGU_BLOCK:(b + 1) * SGU_BLOCK, cols] for b in range(nblk)], axis=1)
            vms.append(jnp.dot(ws[gi], vcat, preferred_element_type=F32))
        return (jnp.dot(h, wu_ref[...], preferred_element_type=F32),
                jnp.dot(h, wz_ref[...], preferred_element_type=F32), vms)

    def gate(r, projected):
        u_acc, z_acc, vms = projected
        u = _gelu(u_acc)
        z = _silu(z_acc)
        for gi in range(groups_per_step):
            cols = slice(gi * SGU_GROUP_DIM, (gi + 1) * SGU_GROUP_DIM)
            vm = vms[gi] + bs_ref[gi]
            for b in range(nblk):
                rows = slice(b * SGU_BLOCK, (b + 1) * SGU_BLOCK)
                vmb = vm[:, b * SGU_GROUP_DIM:(b + 1) * SGU_GROUP_DIM]
                y_ref[r * tr + b * SGU_BLOCK:r * tr + (b + 1) * SGU_BLOCK, cols] = (
                    u[rows, cols] * vmb * z[rows, cols]).astype(BF16)

    _pipelined(tm // tr, project, gate)


def _sgu_gate(h, w_in, vg, mu, rstd, norm_g, w_s, b_s, layer, seq):
    m, d = h.shape
    e = w_in.shape[2] // 3
    groups = e // SGU_GROUP_DIM
    gs = next(g for g in (4, 2, 1) if groups % g == 0)
    tn = gs * SGU_GROUP_DIM
    tm = _tile(seq, 1024, SGU_BLOCK)
    return pl.pallas_call(
        functools.partial(_sgu_gate_kernel, groups_per_step=gs),
        grid=(m // tm, groups // gs),
        in_specs=[
            pl.BlockSpec((tm, d), lambda i, j: (i, 0)),
            pl.BlockSpec((None, d, tn), lambda i, j: (layer, 0, j)),
            pl.BlockSpec((None, d, tn), lambda i, j: (layer, 0, 2 * (e // tn) + j)),
            pl.BlockSpec((tm, tn), lambda i, j: (i, j)),
            pl.BlockSpec((tm, 1), lambda i, j: (i, 0)),
            pl.BlockSpec((tm, 1), lambda i, j: (i, 0)),
            pl.BlockSpec((None, 1, tn), lambda i, j: (layer, 0, j)),
            pl.BlockSpec((None, gs, SGU_BLOCK, SGU_BLOCK), lambda i, j: (layer, j, 0, 0)),
            pl.BlockSpec((None, gs, SGU_BLOCK, 1), lambda i, j: (layer, j, 0, 0)),
        ],
        out_specs=pl.BlockSpec((tm, tn), lambda i, j: (i, j)),
        out_shape=jax.ShapeDtypeStruct((m, e), BF16),
        compiler_params=_params("arbitrary", "arbitrary"),
        name="sgu_gate",
    )(h, w_in, w_in, vg, mu, rstd, norm_g, w_s, b_s)


def _out_proj_kernel(y_ref, w_ref, x_ref, g_ref, gate_ref, *rest, y_transposed, emit_h):
    if emit_h:
        next_g_ref, next_shift_ref, next_scale_ref, o_ref, h_ref = rest
    else:
        (o_ref,) = rest
    tm = o_ref.shape[0]
    tr = min(ROW_BLOCK, tm)

    def project(r):
        if y_transposed:
            return lax.dot_general(y_ref[:, r * tr:(r + 1) * tr], w_ref[...], _TN_DIMS,
                                   preferred_element_type=F32)
        return jnp.dot(y_ref[r * tr:(r + 1) * tr, :], w_ref[...], preferred_element_type=F32)

    def residual(r, acc):
        rows = slice(r * tr, (r + 1) * tr)
        ms = jnp.mean(acc * acc, axis=-1, keepdims=True)
        yn = acc * lax.rsqrt(ms + NORM_EPS) * g_ref[...]
        x_new = x_ref[rows, :] + gate_ref[...] * yn
        o_ref[rows, :] = x_new
        if emit_h:
            h_ref[rows, :] = _modulated_norm(x_new, next_g_ref[...], next_shift_ref[...],
                                             next_scale_ref[...]).astype(BF16)

    _pipelined(tm // tr, project, residual)


def _out_proj(y, w_out, mixer_layer, x, post_g, pre_g, mod, layer, seq, emit_h):
    m, d = x.shape
    k = w_out.shape[1]
    if y.ndim == 2:
        tm = _tile(seq, 512)
        y_spec = pl.BlockSpec((tm, k), lambda i: (i, 0))
    else:
        tm = y.shape[4]
        nq = seq // tm

        def pair_major(i):
            j = i % nq
            lower = j < nq // 2
            return (i // nq, jnp.where(lower, j, nq - 1 - j), jnp.where(lower, 0, 1), 0, 0)

        y_spec = pl.BlockSpec((None, None, None, k, tm), pair_major)
    tpb = seq // tm
    row = pl.BlockSpec((tm, d), lambda i: (i, 0))
    _, _, gate_spec = _mod_specs(layer, tpb, d)
    in_specs = [y_spec, _layer_spec(mixer_layer, k, d), row, _layer_spec(layer, 1, d), gate_spec]
    args = [y, w_out, x, post_g, mod]
    out_specs = [row]
    out_shape = [jax.ShapeDtypeStruct((m, d), F32)]
    if emit_h:
        next_shift_spec, next_scale_spec, _ = _mod_specs(layer + 1, tpb, d)
        in_specs += [_layer_spec(layer + 1, 1, d), next_shift_spec, next_scale_spec]
        args += [pre_g, mod, mod]
        out_specs.append(row)
        out_shape.append(jax.ShapeDtypeStruct((m, d), BF16))
    outs = pl.pallas_call(
        functools.partial(_out_proj_kernel, y_transposed=y.ndim != 2, emit_h=emit_h),
        grid=(m // tm,),
        in_specs=in_specs,
        out_specs=out_specs,
        out_shape=out_shape,
        compiler_params=_params("arbitrary"),
        name="out_proj",
    )(*args)
    return (outs[0], outs[1]) if emit_h else (outs[0], None)


def _mla_in_kernel(x_ref, g_ref, shift_ref, scale_ref, w_ref, wzt_ref, gq_ref, gkv_ref, cos_ref,
                   sin_ref, cq_ref, ckv_ref, kr_ref, szt_ref):
    h = _modulated_norm(x_ref[...], g_ref[...], shift_ref[...], scale_ref[...]).astype(BF16)
    lat = jnp.dot(h, w_ref[...], preferred_element_type=F32)
    cq = lat[:, :LATENT_PAD]
    ckv = lat[:, LATENT_PAD:2 * LATENT_PAD]
    kr = lat[:, 2 * LATENT_PAD:]
    ms_q = jnp.sum(cq * cq, axis=-1, keepdims=True) * (1.0 / Q_LORA_RANK)
    cq_ref[...] = (cq * lax.rsqrt(ms_q + NORM_EPS) * gq_ref[...]).astype(BF16)
    ms_kv = jnp.mean(ckv * ckv, axis=-1, keepdims=True)
    ckv_ref[...] = (ckv * lax.rsqrt(ms_kv + NORM_EPS) * gkv_ref[...]).astype(BF16)
    kr_ref[...] = _rope_rotate(kr, cos_ref[...], sin_ref[...]).astype(BF16)
    zt = lax.dot_general(wzt_ref[...], h, _NT_DIMS, preferred_element_type=F32)
    szt_ref[...] = _silu(zt).astype(BF16)


def _mla_in(x, pre_g, mod, w_lat, wzt, gq, gkv, cos_t, sin_t, layer, mixer_layer, seq):
    m, d = x.shape
    nlat = w_lat.shape[2]
    width = wzt.shape[1]
    tm = _tile(seq, 512)
    tpb = seq // tm
    row = lambda n: pl.BlockSpec((tm, n), lambda i: (i, 0))
    table = pl.BlockSpec((tm, LANES), lambda i: (i % tpb, 0))
    shift_spec, scale_spec, _ = _mod_specs(layer, tpb, d)
    j = mixer_layer
    return pl.pallas_call(
        _mla_in_kernel,
        grid=(m // tm,),
        in_specs=[
            row(d), _layer_spec(layer, 1, d), shift_spec, scale_spec,
            _layer_spec(j, d, nlat), _layer_spec(j, width, d),
            _layer_spec(j, 1, LATENT_PAD), _layer_spec(j, 1, LATENT_PAD), table, table,
        ],
        out_specs=[row(LATENT_PAD), row(LATENT_PAD), row(LANES),
                   pl.BlockSpec((width, tm), lambda i: (0, i))],
        out_shape=[
            jax.ShapeDtypeStruct((m, LATENT_PAD), BF16),
            jax.ShapeDtypeStruct((m, LATENT_PAD), BF16),
            jax.ShapeDtypeStruct((m, LANES), BF16),
            jax.ShapeDtypeStruct((width, m), BF16),
        ],
        compiler_params=_params("arbitrary"),
        name="mla_in",
    )(x, pre_g, mod, mod, w_lat, wzt, gq, gkv, cos_t, sin_t)


def _mla_up_kernel(cq_ref, ckv_ref, kr_ref, wqt_ref, wk_ref, wvt_ref, cos_ref, sin_ref,
                   qt_ref, k_ref, vt_ref, *, heads, scale, tk):
    ckv = ckv_ref[...]
    qt = lax.dot_general(wqt_ref[...], cq_ref[...], _NT_DIMS, preferred_element_type=F32) * scale
    cos_t = cos_ref[...]
    sin_t = sin_ref[...]
    half = QK_ROPE_DIM // 2
    for hd in range(heads):
        r0 = hd * HEAD_PAD
        r1 = r0 + HEAD_PAD - QK_ROPE_DIM
        qt_ref[r0:r1, :] = qt[r0:r1].astype(BF16)
        x1 = qt[r1:r1 + half]
        x2 = qt[r1 + half:r1 + QK_ROPE_DIM]
        qt_ref[r1:r1 + half, :] = (x1 * cos_t - x2 * sin_t).astype(BF16)
        qt_ref[r1 + half:r1 + QK_ROPE_DIM, :] = (x1 * sin_t + x2 * cos_t).astype(BF16)
    kn = jnp.dot(ckv, wk_ref[...], preferred_element_type=F32)
    kr = kr_ref[...]
    for hd in range(heads):
        c0 = hd * HEAD_PAD
        k_ref[:, c0:c0 + LANES] = kn[:, hd * QK_NOPE_DIM:(hd + 1) * QK_NOPE_DIM].astype(BF16)
        k_ref[:, c0 + LANES:c0 + HEAD_PAD] = kr
    vt = lax.dot_general(wvt_ref[...], ckv, _NT_DIMS, preferred_element_type=F32)
    for cb in range(vt.shape[1] // tk):
        vt_ref[cb] = vt[:, cb * tk:(cb + 1) * tk].astype(BF16)


def _mla_up(cq, ckv, kr, wqt, wk, wvt, cos_tt, sin_tt, layer, seq, heads, tk):
    m = cq.shape[0]
    tm = _tile(seq, 512, tk)
    tpb = seq // tm
    row = lambda n: pl.BlockSpec((tm, n), lambda i: (i, 0))
    whole = lambda a: _layer_spec(layer, a.shape[1], a.shape[2])
    table = pl.BlockSpec((QK_ROPE_DIM // 2, tm), lambda i: (0, i % tpb))
    scale = float((QK_NOPE_DIM + QK_ROPE_DIM) ** -0.5 * 1.4426950408889634)
    return pl.pallas_call(
        functools.partial(_mla_up_kernel, heads=heads, scale=scale, tk=tk),
        grid=(m // tm,),
        in_specs=[row(LATENT_PAD), row(LATENT_PAD), row(LANES), whole(wqt), whole(wk), whole(wvt),
                  table, table],
        out_specs=[
            pl.BlockSpec((heads * HEAD_PAD, tm), lambda i: (0, i)),
            row(heads * HEAD_PAD),
            pl.BlockSpec((tm // tk, heads * V_HEAD_DIM, tk), lambda i: (i, 0, 0)),
        ],
        out_shape=[
            jax.ShapeDtypeStruct((heads * HEAD_PAD, m), BF16),
            jax.ShapeDtypeStruct((m, heads * HEAD_PAD), BF16),
            jax.ShapeDtypeStruct((m // tk, heads * V_HEAD_DIM, tk), BF16),
        ],
        compiler_params=_params("arbitrary"),
        name="mla_up",
    )(cq, ckv, kr, wqt, wk, wvt, cos_tt, sin_tt)


def _attn_kernel(qa_ref, qb_ref, k_ref, vt_ref, sza_ref, szb_ref, o_ref, m_sc, acc_sc,
                 *, tk, nq, heads):
    i = pl.program_id(2)
    tq = 2 * tk
    q_refs = (qa_ref, qb_ref)
    sz_refs = (sza_ref, szb_ref)

    def masked(s):
        visible = (lax.broadcasted_iota(jnp.int32, (tk, tk), 0) // CHUNK
                   <= lax.broadcasted_iota(jnp.int32, (tk, tk), 1) // CHUNK)
        return jnp.where(visible, s, -1e30)

    def score_stage(tile):
        hd, slot, t, nt, lane0, mask = tile
        k = k_ref[t * tk:(t + nt) * tk, hd * HEAD_PAD:(hd + 1) * HEAD_PAD]
        qt = q_refs[slot][hd * HEAD_PAD:(hd + 1) * HEAD_PAD, lane0:]
        s = jnp.dot(k, qt, preferred_element_type=F32)
        if mask == "left":
            s = jnp.concatenate([masked(s[:, :tk]), s[:, tk:]], axis=1)
        elif mask == "all":
            s = masked(s)
        return s, jnp.max(s, axis=0, keepdims=True)

    def exp_stage(tile, s, s_max):
        hd, slot, _, _, lane0, mask = tile
        state = 2 * hd + slot
        if mask == "left":
            m_sc[state] = s_max
            return jnp.exp2(s - s_max).astype(BF16), None
        lanes = pl.ds(lane0, tq - lane0)
        m_prev = m_sc[state, :, lanes]
        m_new = jnp.maximum(m_prev, s_max)
        m_sc[state, :, lanes] = m_new
        return jnp.exp2(s - m_new).astype(BF16), jnp.exp2(m_prev - m_new)

    def value_stage(tile, p, alpha):
        hd, slot, t, nt, lane0, _ = tile
        state = 2 * hd + slot
        rows = slice(hd * V_HEAD_DIM, (hd + 1) * V_HEAD_DIM)
        v_t = jnp.concatenate([vt_ref[t + u, rows, :] for u in range(nt)], axis=1)
        v_aug = jnp.concatenate([v_t, jnp.ones((ONES_ROWS, nt * tk), BF16)], axis=0)
        pv = jnp.dot(v_aug, p, preferred_element_type=F32)
        if alpha is None:
            acc_sc[state] = pv
        else:
            lanes = pl.ds(lane0, tq - lane0)
            acc_sc[state, :, lanes] = alpha * acc_sc[state, :, lanes] + pv

    def finalize(hd, slot):
        acc = acc_sc[2 * hd + slot]
        rows = slice(hd * V_HEAD_DIM, (hd + 1) * V_HEAD_DIM)
        out_t = acc[:V_HEAD_DIM] * (1.0 / acc[V_HEAD_DIM:V_HEAD_DIM + 1])
        o_ref[slot, rows, :] = (out_t * sz_refs[slot][rows, :].astype(F32)).astype(BF16)

    def run_pair(c):
        tiles = []
        last = {}
        for hd in range(heads):
            for slot, blk in ((1, nq - 1 - c), (0, c)):
                tiles.append((hd, slot, 2 * blk, 1, 0, "left"))
                tiles.append((hd, slot, 2 * blk + 1, 1, tk, "all"))
                tiles.extend((hd, slot, t, 1, 0, None) for t in range(2 * blk))
                last[len(tiles) - 1] = (hd, slot)
        scored = {}
        exped = {}
        for n in range(len(tiles) + 2):
            if n < len(tiles):
                scored[n] = score_stage(tiles[n])
            if 0 <= n - 1 < len(tiles):
                exped[n - 1] = exp_stage(tiles[n - 1], *scored.pop(n - 1))
            if 0 <= n - 2 < len(tiles):
                value_stage(tiles[n - 2], *exped.pop(n - 2))
                if n - 2 in last:
                    finalize(*last[n - 2])

    for c in range(nq // 2):
        pl.when(i == c)(functools.partial(run_pair, c))


def _attention(qt, k, vt, szt, batch, seq, heads, tk):
    tq = 2 * tk
    nq = seq // tq
    hps = ATTN_HEADS_PER_STEP if heads % ATTN_HEADS_PER_STEP == 0 else 1
    q_spec = lambda f: pl.BlockSpec((hps * HEAD_PAD, tq), lambda b, h, i: (h, b * nq + f(i)))
    sz_spec = lambda f: pl.BlockSpec((hps * V_HEAD_DIM, tq), lambda b, h, i: (h, b * nq + f(i)))
    first = lambda i: i
    second = lambda i: nq - 1 - i
    return pl.pallas_call(
        functools.partial(_attn_kernel, tk=tk, nq=nq, heads=hps),
        grid=(batch, heads // hps, nq // 2),
        in_specs=[
            q_spec(first), q_spec(second),
            pl.BlockSpec((seq, hps * HEAD_PAD), lambda b, h, i: (b, h)),
            pl.BlockSpec((seq // tk, hps * V_HEAD_DIM, tk), lambda b, h, i: (b, h, 0)),
            sz_spec(first), sz_spec(second),
        ],
        out_specs=pl.BlockSpec((None, None, 2, hps * V_HEAD_DIM, tq),
                               lambda b, h, i: (b, i, 0, h, 0)),
        out_shape=jax.ShapeDtypeStruct((batch, nq // 2, 2, heads * V_HEAD_DIM, tq), BF16),
        scratch_shapes=[
            pltpu.VMEM((2 * hps, 1, tq), F32),
            pltpu.VMEM((2 * hps, V_HEAD_DIM + ONES_ROWS, tq), F32),
        ],
        compiler_params=_params("arbitrary", "arbitrary", "arbitrary"),
        name="mla_attention",
    )(qt, qt, k, vt, szt, szt)


def _rope_tables(seq):
    pos = jnp.arange(seq, dtype=F32)
    inv_freq = ROPE_THETA ** (-jnp.arange(0, QK_ROPE_DIM, 2, dtype=F32) / QK_ROPE_DIM)
    ang = pos[:, None] * inv_freq[None, :]
    cos, sin = jnp.cos(ang), jnp.sin(ang)
    ones = jnp.ones((seq, ROPE_LANE0), F32)
    zeros = jnp.zeros((seq, ROPE_LANE0), F32)
    return (jnp.concatenate([ones, cos, cos], axis=-1),
            jnp.concatenate([zeros, -sin, sin], axis=-1), cos.T, sin.T)


def _mla_weight_layout(w_in, gq, w_uq, w_ukv, heads):
    nb, d, _ = w_in.shape
    o1, o2, o3 = Q_LORA_RANK, Q_LORA_RANK + KV_LORA_RANK, Q_LORA_RANK + KV_LORA_RANK + QK_ROPE_DIM
    zc = lambda n: jnp.zeros((nb, d, n), w_in.dtype)
    w_lat = jnp.concatenate([
        w_in[..., :o1], zc(LATENT_PAD - Q_LORA_RANK),
        w_in[..., o1:o2], zc(LATENT_PAD - KV_LORA_RANK),
        zc(ROPE_LANE0), w_in[..., o2:o3],
    ], axis=-1).astype(BF16)
    wzt = jnp.swapaxes(w_in[..., o3:], 1, 2).astype(BF16)
    gq_l = jnp.pad(gq, ((0, 0), (0, LATENT_PAD - Q_LORA_RANK)))[:, None, :]
    wq = w_uq.reshape(nb, Q_LORA_RANK, heads, QK_NOPE_DIM + QK_ROPE_DIM)
    wq = jnp.concatenate([
        wq[..., :QK_NOPE_DIM],
        jnp.zeros((nb, Q_LORA_RANK, heads, ROPE_LANE0), wq.dtype),
        wq[..., QK_NOPE_DIM:],
    ], axis=-1).reshape(nb, Q_LORA_RANK, heads * HEAD_PAD)
    wqt = jnp.swapaxes(
        jnp.pad(wq, ((0, 0), (0, LATENT_PAD - Q_LORA_RANK), (0, 0))), 1, 2).astype(BF16)
    wkv = w_ukv.reshape(nb, KV_LORA_RANK, heads, QK_NOPE_DIM + V_HEAD_DIM)
    wk = wkv[..., :QK_NOPE_DIM].reshape(nb, KV_LORA_RANK, heads * QK_NOPE_DIM).astype(BF16)
    wvt = jnp.swapaxes(
        wkv[..., QK_NOPE_DIM:].reshape(nb, KV_LORA_RANK, heads * V_HEAD_DIM), 1, 2).astype(BF16)
    return w_lat, wzt, gq_l, wqt, wk, wvt


def kernel(x, c, ada_w, ada_b, pre_g, post_g, sgu_w_in, sgu_norm_g, sgu_w_s, sgu_b_s, sgu_w_out,
           mla_w_in, mla_q_norm_g, mla_kv_norm_g, mla_w_uq, mla_w_ukv, mla_w_out):
    batch, seq, d = x.shape
    depth = ada_w.shape[0]
    heads = mla_w_out.shape[1] // V_HEAD_DIM
    tk = ATTN_KV_TILE
    assert seq % SGU_BLOCK == 0 and seq % (2 * tk) == 0 and d % LANES == 0
    assert sgu_w_in.shape[2] % (3 * SGU_GROUP_DIM) == 0
    assert mla_kv_norm_g.shape[1] == KV_LORA_RANK == LATENT_PAD

    mod = _adaln_mod(c, ada_w, ada_b).reshape(depth, batch, 3, 1, d)
    pre_g3 = pre_g[:, None, :]
    post_g3 = post_g[:, None, :]
    sgu_w_in_l = sgu_w_in.astype(BF16)
    sgu_w_out_l = sgu_w_out.astype(BF16)
    sgu_norm_g3 = sgu_norm_g[:, None, :]
    sgu_b_s4 = sgu_b_s[..., None]
    w_lat, wzt, gq_l, wqt, wk, wvt = _mla_weight_layout(
        mla_w_in, mla_q_norm_g, mla_w_uq, mla_w_ukv, heads)
    gkv_l = mla_kv_norm_g[:, None, :]
    mla_w_out_l = mla_w_out.astype(BF16)
    cos_t, sin_t, cos_tt, sin_tt = _rope_tables(seq)

    xf = x.reshape(batch * seq, d)
    h = None
    for i in range(depth):
        j = i // 2
        if i % 2 == 0:
            norm = (xf, pre_g3, mod, i) if h is None else None
            vg, mu, rstd, h = _sgu_v(h, sgu_w_in_l, j, seq, norm)
            y = _sgu_gate(h, sgu_w_in_l, vg, mu, rstd, sgu_norm_g3, sgu_w_s, sgu_b_s4, j, seq)
            w_out = sgu_w_out_l
        else:
            cq, ckv, kr, szt = _mla_in(xf, pre_g3, mod, w_lat, wzt, gq_l, gkv_l, cos_t, sin_t,
                                       i, j, seq)
            qt, k, vt = _mla_up(cq, ckv, kr, wqt, wk, wvt, cos_tt, sin_tt, j, seq, heads, tk)
            y = _attention(qt, k, vt, szt, batch, seq, heads, tk)
            w_out = mla_w_out_l
        next_is_sgu = i + 1 < depth and (i + 1) % 2 == 0
        xf, h = _out_proj(y, w_out, j, xf, post_g3, pre_g3, mod, i, seq, emit_h=next_is_sgu)
    return xf.reshape(batch, seq, d)
```

```python
import functools

import jax
import jax.numpy as jnp
from jax import lax
from jax.experimental import pallas as pl
from jax.experimental.pallas import tpu as pltpu

F32 = jnp.float32
BF16 = jnp.bfloat16

NORM_EPS = 1e-6
CHUNK = 64
SGU_BLOCK = 128
SGU_GROUP_DIM = 256
Q_LORA_RANK = 448
KV_LORA_RANK = 512
QK_NOPE_DIM = 128
QK_ROPE_DIM = 64
V_HEAD_DIM = 128
ROPE_THETA = 10000.0
LANES = 128
LATENT_PAD = 512
HEAD_PAD = 2 * LANES
Q_HEAD_ROWS = QK_NOPE_DIM + QK_ROPE_DIM
ROPE_LANE0 = LANES - QK_ROPE_DIM
ATTN_KV_TILE = 256
ATTN_HEADS_PER_STEP = 2
ONES_ROWS = 16
VMEM_LIMIT_BYTES = 60 * 1024 * 1024
_NN_DIMS = (((1,), (0,)), ((), ()))
_NT_DIMS = (((1,), (1,)), ((), ()))
_TN_DIMS = (((0,), (0,)), ((), ()))


def _tile(n, pref, mult=8):
    t = min(n, pref)
    while t > mult and (n % t or t % mult):
        t -= mult
    return t if n % t == 0 else n


def _params(*semantics):
    return pltpu.CompilerParams(dimension_semantics=semantics, vmem_limit_bytes=VMEM_LIMIT_BYTES)


def _gelu(x):
    return 0.5 * x * (1.0 + lax.erf(x * (0.5 ** 0.5)))


def _silu(x):
    return x * jax.nn.sigmoid(x)


def _modulated_norm(x, g, shift, scale):
    ms = jnp.mean(x * x, axis=-1, keepdims=True)
    y = x * lax.rsqrt(ms + NORM_EPS) * g
    return y * (1.0 + scale) + shift


def _rope_rotate(x, cos_t, sin_t):
    lane = lax.broadcasted_iota(jnp.int32, x.shape, 1)
    half = QK_ROPE_DIM // 2
    partner = jnp.where(lane < ROPE_LANE0 + half,
                        pltpu.roll(x, LANES - half, 1),
                        pltpu.roll(x, half, 1))
    return x * cos_t + partner * sin_t


ROW_BLOCK = 256


def _pipelined(n_blocks, produce, consume):
    pending = None
    for r in range(n_blocks):
        current = produce(r)
        if pending is not None:
            consume(r - 1, pending)
        pending = current
    consume(n_blocks - 1, pending)


def _layer_spec(layer, rows, cols):
    return pl.BlockSpec((None, rows, cols), lambda *_: (layer, 0, 0), pipeline_mode=pl.Buffered(1))


def _mod_specs(layer, tiles_per_batch, d):
    def spec(k):
        return pl.BlockSpec((None, None, None, 1, d),
                            lambda m, *_: (layer, m // tiles_per_batch, k, 0, 0))
    return spec(0), spec(1), spec(2)


def _mod_kernel(c_ref, w_ref, b_ref, o_ref):
    cond = _silu(c_ref[...]).astype(BF16)
    o_ref[0] = jnp.dot(cond, w_ref[0].astype(BF16), preferred_element_type=F32) + b_ref[0]


def _adaln_mod(c, ada_w, ada_b):
    depth, d, n = ada_w.shape
    b = c.shape[0]
    tn = _tile(n, 1536, LANES)
    return pl.pallas_call(
        _mod_kernel,
        grid=(depth, n // tn),
        in_specs=[
            pl.BlockSpec((b, d), lambda i, j: (0, 0)),
            pl.BlockSpec((1, d, tn), lambda i, j: (i, 0, j)),
            pl.BlockSpec((1, 1, tn), lambda i, j: (i, 0, j)),
        ],
        out_specs=pl.BlockSpec((1, b, tn), lambda i, j: (i, 0, j)),
        out_shape=jax.ShapeDtypeStruct((depth, b, n), F32),
        compiler_params=_params("arbitrary", "arbitrary"),
        name="adaln_mod",
    )(c, ada_w, ada_b.reshape(depth, 1, n))


def _sgu_v_kernel(x_ref, g_ref, shift_ref, scale_ref, w_ref, h_ref, vg_ref, mu_ref, rstd_ref):
    tm = x_ref.shape[0]
    tr = min(ROW_BLOCK, tm)

    def project(r):
        rows = slice(r * tr, (r + 1) * tr)
        h = _modulated_norm(x_ref[rows, :], g_ref[...], shift_ref[...], scale_ref[...]).astype(BF16)
        h_ref[rows, :] = h
        return jnp.dot(h, w_ref[...], preferred_element_type=F32)

    def activate(r, acc):
        rows = slice(r * tr, (r + 1) * tr)
        v = _gelu(acc)
        vg_ref[rows, :] = v.astype(BF16)
        mu = jnp.mean(v, axis=-1, keepdims=True)
        var = jnp.mean(v * v, axis=-1, keepdims=True) - mu * mu
        mu_ref[rows, :] = mu
        rstd_ref[rows, :] = lax.rsqrt(var + NORM_EPS)

    _pipelined(tm // tr, project, activate)


def _sgu_v(x, pre_g, mod, w_in, layer, mixer_layer, seq):
    m, d = x.shape
    e = w_in.shape[2] // 3
    tm = _tile(seq, 512, SGU_BLOCK)
    row = pl.BlockSpec((tm, d), lambda i: (i, 0))
    stat = pl.BlockSpec((tm, 1), lambda i: (i, 0))
    shift_spec, scale_spec, _ = _mod_specs(layer, seq // tm, d)
    return pl.pallas_call(
        _sgu_v_kernel,
        grid=(m // tm,),
        in_specs=[
            row, _layer_spec(layer, 1, d), shift_spec, scale_spec,
            pl.BlockSpec((None, d, e), lambda i: (mixer_layer, 0, 1), pipeline_mode=pl.Buffered(1)),
        ],
        out_specs=[row, pl.BlockSpec((tm, e), lambda i: (i, 0)), stat, stat],
        out_shape=[
            jax.ShapeDtypeStruct((m, d), BF16),
            jax.ShapeDtypeStruct((m, e), BF16),
            jax.ShapeDtypeStruct((m, 1), F32),
            jax.ShapeDtypeStruct((m, 1), F32),
        ],
        compiler_params=_params("arbitrary"),
        name="sgu_v",
    )(x, pre_g, mod, mod, w_in)


def _sgu_gate_kernel(h_ref, wu_ref, wz_ref, vg_ref, mu_ref, rstd_ref, ng_ref, ws_ref, bs_ref,
                     y_ref, *, groups_per_step):
    tm = h_ref.shape[0]
    tr = min(ROW_BLOCK, tm)
    nblk = tr // SGU_BLOCK
    t_chunk = lax.broadcasted_iota(jnp.int32, (SGU_BLOCK, SGU_BLOCK), 0) // CHUNK
    s_chunk = lax.broadcasted_iota(jnp.int32, (SGU_BLOCK, SGU_BLOCK), 1) // CHUNK
    causal = s_chunk <= t_chunk
    ws = [jnp.where(causal, ws_ref[gi], 0.0).astype(BF16) for gi in range(groups_per_step)]

    def project(r):
        rows = slice(r * tr, (r + 1) * tr)
        h = h_ref[rows, :]
        vn = ((vg_ref[rows, :].astype(F32) - mu_ref[rows, :]) * rstd_ref[rows, :]
              * ng_ref[...]).astype(BF16)
        vms = []
        for gi in range(groups_per_step):
            cols = slice(gi * SGU_GROUP_DIM, (gi + 1) * SGU_GROUP_DIM)
            vcat = jnp.concatenate(
                [vn[b * SGU_BLOCK:(b + 1) * SGU_BLOCK, cols] for b in range(nblk)], axis=1)
            vms.append(jnp.dot(ws[gi], vcat, preferred_element_type=F32))
        return (jnp.dot(h, wu_ref[...], preferred_element_type=F32),
                jnp.dot(h, wz_ref[...], preferred_element_type=F32), vms)

    def gate(r, projected):
        u_acc, z_acc, vms = projected
        u = _gelu(u_acc)
        z = _silu(z_acc)
        for gi in range(groups_per_step):
            cols = slice(gi * SGU_GROUP_DIM, (gi + 1) * SGU_GROUP_DIM)
            vm = vms[gi] + bs_ref[gi]
            for b in range(nblk):
                rows = slice(b * SGU_BLOCK, (b + 1) * SGU_BLOCK)
                vmb = vm[:, b * SGU_GROUP_DIM:(b + 1) * SGU_GROUP_DIM]
                y_ref[r * tr + b * SGU_BLOCK:r * tr + (b + 1) * SGU_BLOCK, cols] = (
                    u[rows, cols] * vmb * z[rows, cols]).astype(BF16)

    _pipelined(tm // tr, project, gate)


def _sgu_gate(h, w_in, vg, mu, rstd, norm_g, w_s, b_s, layer, seq):
    m, d = h.shape
    e = w_in.shape[2] // 3
    groups = e // SGU_GROUP_DIM
    gs = next(g for g in (4, 2, 1) if groups % g == 0)
    tn = gs * SGU_GROUP_DIM
    tm = _tile(seq, 1024, SGU_BLOCK)
    return pl.pallas_call(
        functools.partial(_sgu_gate_kernel, groups_per_step=gs),
        grid=(m // tm, groups // gs),
        in_specs=[
            pl.BlockSpec((tm, d), lambda i, j: (i, 0)),
            pl.BlockSpec((None, d, tn), lambda i, j: (layer, 0, j)),
            pl.BlockSpec((None, d, tn), lambda i, j: (layer, 0, 2 * (e // tn) + j)),
            pl.BlockSpec((tm, tn), lambda i, j: (i, j)),
            pl.BlockSpec((tm, 1), lambda i, j: (i, 0)),
            pl.BlockSpec((tm, 1), lambda i, j: (i, 0)),
            pl.BlockSpec((None, 1, tn), lambda i, j: (layer, 0, j)),
            pl.BlockSpec((None, gs, SGU_BLOCK, SGU_BLOCK), lambda i, j: (layer, j, 0, 0)),
            pl.BlockSpec((None, gs, SGU_BLOCK, 1), lambda i, j: (layer, j, 0, 0)),
        ],
        out_specs=pl.BlockSpec((tm, tn), lambda i, j: (i, j)),
        out_shape=jax.ShapeDtypeStruct((m, e), BF16),
        compiler_params=_params("arbitrary", "arbitrary"),
        name="sgu_gate",
    )(h, w_in, w_in, vg, mu, rstd, norm_g, w_s, b_s)


def _out_proj_kernel(y_ref, w_ref, x_ref, g_ref, gate_ref, o_ref, *, y_transposed):
    tm = o_ref.shape[0]
    tr = min(ROW_BLOCK, tm)

    def project(r):
        if y_transposed:
            return lax.dot_general(y_ref[:, r * tr:(r + 1) * tr], w_ref[...], _TN_DIMS,
                                   preferred_element_type=F32)
        return jnp.dot(y_ref[r * tr:(r + 1) * tr, :], w_ref[...], preferred_element_type=F32)

    def residual(r, acc):
        rows = slice(r * tr, (r + 1) * tr)
        ms = jnp.mean(acc * acc, axis=-1, keepdims=True)
        yn = acc * lax.rsqrt(ms + NORM_EPS) * g_ref[...]
        o_ref[rows, :] = x_ref[rows, :] + gate_ref[...] * yn

    _pipelined(tm // tr, project, residual)


def _out_proj(y, w_out, mixer_layer, x, post_g, mod, layer, seq):
    m, d = x.shape
    k = w_out.shape[1]
    if y.ndim == 2:
        tm = _tile(seq, 512)
        y_spec = pl.BlockSpec((tm, k), lambda i: (i, 0))
    else:
        tm = y.shape[4]
        nq = seq // tm

        def pair_major(i):
            j = i % nq
            lower = j < nq // 2
            return (i // nq, jnp.where(lower, j, nq - 1 - j), jnp.where(lower, 0, 1), 0, 0)

        y_spec = pl.BlockSpec((None, None, None, k, tm), pair_major)
    row = pl.BlockSpec((tm, d), lambda i: (i, 0))
    _, _, gate_spec = _mod_specs(layer, seq // tm, d)
    return pl.pallas_call(
        functools.partial(_out_proj_kernel, y_transposed=y.ndim != 2),
        grid=(m // tm,),
        in_specs=[y_spec, _layer_spec(mixer_layer, k, d), row, _layer_spec(layer, 1, d), gate_spec],
        out_specs=row,
        out_shape=jax.ShapeDtypeStruct((m, d), F32),
        compiler_params=_params("arbitrary"),
        name="out_proj",
    )(y, w_out, x, post_g, mod)


def _mla_in_kernel(x_ref, g_ref, shift_ref, scale_ref, w_ref, wzt_ref, gq_ref, gkv_ref, cos_ref,
                   sin_ref, cq_ref, ckv_ref, kr_ref, szt_ref):
    h = _modulated_norm(x_ref[...], g_ref[...], shift_ref[...], scale_ref[...]).astype(BF16)
    lat = jnp.dot(h, w_ref[...], preferred_element_type=F32)
    cq = lat[:, :LATENT_PAD]
    ckv = lat[:, LATENT_PAD:2 * LATENT_PAD]
    kr = lat[:, 2 * LATENT_PAD:]
    ms_q = jnp.sum(cq * cq, axis=-1, keepdims=True) * (1.0 / Q_LORA_RANK)
    cq_ref[...] = (cq * lax.rsqrt(ms_q + NORM_EPS) * gq_ref[...]).astype(BF16)
    ms_kv = jnp.mean(ckv * ckv, axis=-1, keepdims=True)
    ckv_ref[...] = (ckv * lax.rsqrt(ms_kv + NORM_EPS) * gkv_ref[...]).astype(BF16)
    kr_ref[...] = _rope_rotate(kr, cos_ref[...], sin_ref[...]).astype(BF16)
    zt = lax.dot_general(wzt_ref[...], h, _NT_DIMS, preferred_element_type=F32)
    szt_ref[...] = _silu(zt).astype(BF16)


def _mla_in(x, pre_g, mod, w_lat, wzt, gq, gkv, cos_t, sin_t, layer, mixer_layer, seq):
    m, d = x.shape
    nlat = w_lat.shape[2]
    width = wzt.shape[1]
    tm = _tile(seq, 512)
    tpb = seq // tm
    row = lambda n: pl.BlockSpec((tm, n), lambda i: (i, 0))
    table = pl.BlockSpec((tm, LANES), lambda i: (i % tpb, 0))
    shift_spec, scale_spec, _ = _mod_specs(layer, tpb, d)
    j = mixer_layer
    return pl.pallas_call(
        _mla_in_kernel,
        grid=(m // tm,),
        in_specs=[
            row(d), _layer_spec(layer, 1, d), shift_spec, scale_spec,
            _layer_spec(j, d, nlat), _layer_spec(j, width, d),
            _layer_spec(j, 1, LATENT_PAD), _layer_spec(j, 1, LATENT_PAD), table, table,
        ],
        out_specs=[row(LATENT_PAD), row(LATENT_PAD), row(LANES),
                   pl.BlockSpec((width, tm), lambda i: (0, i))],
        out_shape=[
            jax.ShapeDtypeStruct((m, LATENT_PAD), BF16),
            jax.ShapeDtypeStruct((m, LATENT_PAD), BF16),
            jax.ShapeDtypeStruct((m, LANES), BF16),
            jax.ShapeDtypeStruct((width, m), BF16),
        ],
        compiler_params=_params("arbitrary"),
        name="mla_in",
    )(x, pre_g, mod, mod, w_lat, wzt, gq, gkv, cos_t, sin_t)


def _mla_up_kernel(cq_ref, ckv_ref, kr_ref, wqt_ref, wk_ref, wvt_ref, cos_ref, sin_ref,
                   qt_ref, k_ref, vt_ref, *, heads, scale, tk):
    ckv = ckv_ref[...]
    qt = lax.dot_general(wqt_ref[...], cq_ref[...], _NT_DIMS, preferred_element_type=F32) * scale
    cos_t = cos_ref[...]
    sin_t = sin_ref[...]
    half = QK_ROPE_DIM // 2
    for hd in range(heads):
        r0 = hd * Q_HEAD_ROWS
        r1 = r0 + QK_NOPE_DIM
        qt_ref[r0:r1, :] = qt[r0:r1].astype(BF16)
        x1 = qt[r1:r1 + half]
        x2 = qt[r1 + half:r1 + QK_ROPE_DIM]
        qt_ref[r1:r1 + half, :] = (x1 * cos_t - x2 * sin_t).astype(BF16)
        qt_ref[r1 + half:r1 + QK_ROPE_DIM, :] = (x1 * sin_t + x2 * cos_t).astype(BF16)
    kn = jnp.dot(ckv, wk_ref[...], preferred_element_type=F32)
    kr = kr_ref[...]
    for hd in range(heads):
        c0 = hd * HEAD_PAD
        k_ref[:, c0:c0 + LANES] = kn[:, hd * QK_NOPE_DIM:(hd + 1) * QK_NOPE_DIM].astype(BF16)
        k_ref[:, c0 + LANES:c0 + HEAD_PAD] = kr
    vt = lax.dot_general(wvt_ref[...], ckv, _NT_DIMS, preferred_element_type=F32)
    for cb in range(vt.shape[1] // tk):
        vt_ref[cb] = vt[:, cb * tk:(cb + 1) * tk].astype(BF16)


def _mla_up(cq, ckv, kr, wqt, wk, wvt, cos_tt, sin_tt, layer, seq, heads, tk):
    m = cq.shape[0]
    tm = _tile(seq, 512, tk)
    tpb = seq // tm
    row = lambda n: pl.BlockSpec((tm, n), lambda i: (i, 0))
    whole = lambda a: _layer_spec(layer, a.shape[1], a.shape[2])
    table = pl.BlockSpec((QK_ROPE_DIM // 2, tm), lambda i: (0, i % tpb))
    scale = float((QK_NOPE_DIM + QK_ROPE_DIM) ** -0.5 * 1.4426950408889634)
    return pl.pallas_call(
        functools.partial(_mla_up_kernel, heads=heads, scale=scale, tk=tk),
        grid=(m // tm,),
        in_specs=[row(LATENT_PAD), row(LATENT_PAD), row(LANES), whole(wqt), whole(wk), whole(wvt),
                  table, table],
        out_specs=[
            pl.BlockSpec((heads * Q_HEAD_ROWS, tm), lambda i: (0, i)),
            row(heads * HEAD_PAD),
            pl.BlockSpec((tm // tk, heads * V_HEAD_DIM, tk), lambda i: (i, 0, 0)),
        ],
        out_shape=[
            jax.ShapeDtypeStruct((heads * Q_HEAD_ROWS, m), BF16),
            jax.ShapeDtypeStruct((m, heads * HEAD_PAD), BF16),
            jax.ShapeDtypeStruct((m // tk, heads * V_HEAD_DIM, tk), BF16),
        ],
        compiler_params=_params("arbitrary"),
        name="mla_up",
    )(cq, ckv, kr, wqt, wk, wvt, cos_tt, sin_tt)


def _attn_kernel(qa_ref, qb_ref, k_ref, vt_ref, sza_ref, szb_ref, o_ref, m_sc, acc_sc,
                 *, tk, nq, heads):
    i = pl.program_id(2)
    tq = 2 * tk
    q_refs = (qa_ref, qb_ref)
    sz_refs = (sza_ref, szb_ref)

    def masked(s):
        visible = (lax.broadcasted_iota(jnp.int32, (tk, tk), 0) // CHUNK
                   <= lax.broadcasted_iota(jnp.int32, (tk, tk), 1) // CHUNK)
        return jnp.where(visible, s, -1e30)

    def score_stage(tile):
        hd, slot, t, nt, lane0, mask = tile
        k = k_ref[t * tk:(t + nt) * tk, hd * HEAD_PAD:(hd + 1) * HEAD_PAD]
        q = q_refs[slot][hd * Q_HEAD_ROWS:(hd + 1) * Q_HEAD_ROWS, lane0:]
        qt = jnp.concatenate([q[:QK_NOPE_DIM], jnp.zeros((ROPE_LANE0, q.shape[1]), BF16),
                              q[QK_NOPE_DIM:]], axis=0)
        s = jnp.dot(k, qt, preferred_element_type=F32)
        if mask == "left":
            s = jnp.concatenate([masked(s[:, :tk]), s[:, tk:]], axis=1)
        elif mask == "all":
            s = masked(s)
        return s, jnp.max(s, axis=0, keepdims=True)

    def exp_stage(tile, s, s_max):
        hd, slot, _, _, lane0, mask = tile
        state = 2 * hd + slot
        if mask == "left":
            m_sc[state] = s_max
            return jnp.exp2(s - s_max).astype(BF16), None
        lanes = pl.ds(lane0, tq - lane0)
        m_prev = m_sc[state, :, lanes]
        m_new = jnp.maximum(m_prev, s_max)
        m_sc[state, :, lanes] = m_new
        return jnp.exp2(s - m_new).astype(BF16), jnp.exp2(m_prev - m_new)

    def value_stage(tile, p, alpha):
        hd, slot, t, nt, lane0, _ = tile
        state = 2 * hd + slot
        rows = slice(hd * V_HEAD_DIM, (hd + 1) * V_HEAD_DIM)
        v_t = jnp.concatenate([vt_ref[t + u, rows, :] for u in range(nt)], axis=1)
        v_aug = jnp.concatenate([v_t, jnp.ones((ONES_ROWS, nt * tk), BF16)], axis=0)
        pv = jnp.dot(v_aug, p, preferred_element_type=F32)
        if alpha is None:
            acc_sc[state] = pv
        else:
            lanes = pl.ds(lane0, tq - lane0)
            acc_sc[state, :, lanes] = alpha * acc_sc[state, :, lanes] + pv

    def finalize(hd, slot):
        acc = acc_sc[2 * hd + slot]
        rows = slice(hd * V_HEAD_DIM, (hd + 1) * V_HEAD_DIM)
        out_t = acc[:V_HEAD_DIM] * (1.0 / acc[V_HEAD_DIM:V_HEAD_DIM + 1])
        o_ref[slot, rows, :] = (out_t * sz_refs[slot][rows, :].astype(F32)).astype(BF16)

    def run_pair(c):
        tiles = []
        last = {}
        for hd in range(heads):
            for slot, blk in ((1, nq - 1 - c), (0, c)):
                tiles.append((hd, slot, 2 * blk, 1, 0, "left"))
                tiles.append((hd, slot, 2 * blk + 1, 1, tk, "all"))
                tiles.extend((hd, slot, t, 1, 0, None) for t in range(2 * blk))
                last[len(tiles) - 1] = (hd, slot)
        scored = {}
        exped = {}
        for n in range(len(tiles) + 2):
            if n < len(tiles):
                scored[n] = score_stage(tiles[n])
            if 0 <= n - 1 < len(tiles):
                exped[n - 1] = exp_stage(tiles[n - 1], *scored.pop(n - 1))
            if 0 <= n - 2 < len(tiles):
                value_stage(tiles[n - 2], *exped.pop(n - 2))
                if n - 2 in last:
                    finalize(*last[n - 2])

    for c in range(nq // 2):
        pl.when(i == c)(functools.partial(run_pair, c))


def _attention(qt, k, vt, szt, batch, seq, heads, tk):
    tq = 2 * tk
    nq = seq // tq
    hps = ATTN_HEADS_PER_STEP if heads % ATTN_HEADS_PER_STEP == 0 else 1
    q_spec = lambda f: pl.BlockSpec((hps * Q_HEAD_ROWS, tq), lambda b, h, i: (h, b * nq + f(i)))
    sz_spec = lambda f: pl.BlockSpec((hps * V_HEAD_DIM, tq), lambda b, h, i: (h, b * nq + f(i)))
    first = lambda i: i
    second = lambda i: nq - 1 - i
    return pl.pallas_call(
        functools.partial(_attn_kernel, tk=tk, nq=nq, heads=hps),
        grid=(batch, heads // hps, nq // 2),
        in_specs=[
            q_spec(first), q_spec(second),
            pl.BlockSpec((seq, hps * HEAD_PAD), lambda b, h, i: (b, h)),
            pl.BlockSpec((seq // tk, hps * V_HEAD_DIM, tk), lambda b, h, i: (b, h, 0)),
            sz_spec(first), sz_spec(second),
        ],
        out_specs=pl.BlockSpec((None, None, 2, hps * V_HEAD_DIM, tq),
                               lambda b, h, i: (b, i, 0, h, 0)),
        out_shape=jax.ShapeDtypeStruct((batch, nq // 2, 2, heads * V_HEAD_DIM, tq), BF16),
        scratch_shapes=[
            pltpu.VMEM((2 * hps, 1, tq), F32),
            pltpu.VMEM((2 * hps, V_HEAD_DIM + ONES_ROWS, tq), F32),
        ],
        compiler_params=_params("arbitrary", "arbitrary", "arbitrary"),
        name="mla_attention",
    )(qt, qt, k, vt, szt, szt)


def _rope_tables(seq):
    pos = jnp.arange(seq, dtype=F32)
    inv_freq = ROPE_THETA ** (-jnp.arange(0, QK_ROPE_DIM, 2, dtype=F32) / QK_ROPE_DIM)
    ang = pos[:, None] * inv_freq[None, :]
    cos, sin = jnp.cos(ang), jnp.sin(ang)
    ones = jnp.ones((seq, ROPE_LANE0), F32)
    zeros = jnp.zeros((seq, ROPE_LANE0), F32)
    return (jnp.concatenate([ones, cos, cos], axis=-1),
            jnp.concatenate([zeros, -sin, sin], axis=-1), cos.T, sin.T)


def _mla_weight_layout(w_in, gq, w_uq, w_ukv, heads):
    nb, d, _ = w_in.shape
    o1, o2, o3 = Q_LORA_RANK, Q_LORA_RANK + KV_LORA_RANK, Q_LORA_RANK + KV_LORA_RANK + QK_ROPE_DIM
    zc = lambda n: jnp.zeros((nb, d, n), w_in.dtype)
    w_lat = jnp.concatenate([
        w_in[..., :o1], zc(LATENT_PAD - Q_LORA_RANK),
        w_in[..., o1:o2], zc(LATENT_PAD - KV_LORA_RANK),
        zc(ROPE_LANE0), w_in[..., o2:o3],
    ], axis=-1).astype(BF16)
    wzt = jnp.swapaxes(w_in[..., o3:], 1, 2).astype(BF16)
    gq_l = jnp.pad(gq, ((0, 0), (0, LATENT_PAD - Q_LORA_RANK)))[:, None, :]
    wqt = jnp.swapaxes(
        jnp.pad(w_uq, ((0, 0), (0, LATENT_PAD - Q_LORA_RANK), (0, 0))), 1, 2).astype(BF16)
    wkv = w_ukv.reshape(nb, KV_LORA_RANK, heads, QK_NOPE_DIM + V_HEAD_DIM)
    wk = wkv[..., :QK_NOPE_DIM].reshape(nb, KV_LORA_RANK, heads * QK_NOPE_DIM).astype(BF16)
    wvt = jnp.swapaxes(
        wkv[..., QK_NOPE_DIM:].reshape(nb, KV_LORA_RANK, heads * V_HEAD_DIM), 1, 2).astype(BF16)
    return w_lat, wzt, gq_l, wqt, wk, wvt


def kernel(x, c, ada_w, ada_b, pre_g, post_g, sgu_w_in, sgu_norm_g, sgu_w_s, sgu_b_s, sgu_w_out,
           mla_w_in, mla_q_norm_g, mla_kv_norm_g, mla_w_uq, mla_w_ukv, mla_w_out):
    batch, seq, d = x.shape
    depth = ada_w.shape[0]
    heads = mla_w_out.shape[1] // V_HEAD_DIM
    tk = ATTN_KV_TILE
    assert seq % SGU_BLOCK == 0 and seq % (2 * tk) == 0 and d % LANES == 0
    assert sgu_w_in.shape[2] % (3 * SGU_GROUP_DIM) == 0
    assert mla_kv_norm_g.shape[1] == KV_LORA_RANK == LATENT_PAD

    mod = _adaln_mod(c, ada_w, ada_b).reshape(depth, batch, 3, 1, d)
    pre_g3 = pre_g[:, None, :]
    post_g3 = post_g[:, None, :]
    sgu_w_in_l = sgu_w_in.astype(BF16)
    sgu_w_out_l = sgu_w_out.astype(BF16)
    sgu_norm_g3 = sgu_norm_g[:, None, :]
    sgu_b_s4 = sgu_b_s[..., None]
    w_lat, wzt, gq_l, wqt, wk, wvt = _mla_weight_layout(
        mla_w_in, mla_q_norm_g, mla_w_uq, mla_w_ukv, heads)
    gkv_l = mla_kv_norm_g[:, None, :]
    mla_w_out_l = mla_w_out.astype(BF16)
    cos_t, sin_t, cos_tt, sin_tt = _rope_tables(seq)

    xf = x.reshape(batch * seq, d)
    for i in range(depth):
        j = i // 2
        if i % 2 == 0:
            h, vg, mu, rstd = _sgu_v(xf, pre_g3, mod, sgu_w_in_l, i, j, seq)
            y = _sgu_gate(h, sgu_w_in_l, vg, mu, rstd, sgu_norm_g3, sgu_w_s, sgu_b_s4, j, seq)
            w_out = sgu_w_out_l
        else:
            cq, ckv, kr, szt = _mla_in(xf, pre_g3, mod, w_lat, wzt, gq_l, gkv_l, cos_t, sin_t,
                                       i, j, seq)
            qt, k, vt = _mla_up(cq, ckv, kr, wqt, wk, wvt, cos_tt, sin_tt, j, seq, heads, tk)
            y = _attention(qt, k, vt, szt, batch, seq, heads, tk)
            w_out = mla_w_out_l
        xf = _out_proj(y, w_out, j, xf, post_g3, mod, i, seq)
    return xf.reshape(batch, seq, d)
```

```python
import functools

import jax
import jax.numpy as jnp
from jax import lax
from jax.experimental import pallas as pl
from jax.experimental.pallas import tpu as pltpu

F32 = jnp.float32
BF16 = jnp.bfloat16

NORM_EPS = 1e-6
CHUNK = 64
SGU_BLOCK = 128
SGU_GROUP_DIM = 256
Q_LORA_RANK = 448
KV_LORA_RANK = 512
QK_NOPE_DIM = 128
QK_ROPE_DIM = 64
V_HEAD_DIM = 128
ROPE_THETA = 10000.0
LANES = 128
LATENT_PAD = 512
HEAD_PAD = 2 * LANES
Q_HEAD_ROWS = QK_NOPE_DIM + QK_ROPE_DIM
ROPE_LANE0 = LANES - QK_ROPE_DIM
ATTN_KV_TILE = 256
ATTN_HEADS_PER_STEP = 2
ONES_ROWS = 16
VMEM_LIMIT_BYTES = 60 * 1024 * 1024
_NN_DIMS = (((1,), (0,)), ((), ()))
_NT_DIMS = (((1,), (1,)), ((), ()))
_TN_DIMS = (((0,), (0,)), ((), ()))


def _tile(n, pref, mult=8):
    t = min(n, pref)
    while t > mult and (n % t or t % mult):
        t -= mult
    return t if n % t == 0 else n


def _params(*semantics):
    return pltpu.CompilerParams(dimension_semantics=semantics, vmem_limit_bytes=VMEM_LIMIT_BYTES)


def _gelu(x):
    return 0.5 * x * (1.0 + lax.erf(x * (0.5 ** 0.5)))


def _silu(x):
    return x * jax.nn.sigmoid(x)


def _modulated_norm(x, g, shift, scale):
    ms = jnp.mean(x * x, axis=-1, keepdims=True)
    y = x * lax.rsqrt(ms + NORM_EPS) * g
    return y * (1.0 + scale) + shift


def _rope_rotate(x, cos_t, sin_t):
    lane = lax.broadcasted_iota(jnp.int32, x.shape, 1)
    half = QK_ROPE_DIM // 2
    partner = jnp.where(lane < ROPE_LANE0 + half,
                        pltpu.roll(x, LANES - half, 1),
                        pltpu.roll(x, half, 1))
    return x * cos_t + partner * sin_t


ROW_BLOCK = 256
TOKEN_TILE = 512
SGU_GATE_TOKEN_TILE = 1024
MOD_COL_TILE = 1536


def _pipelined(n_blocks, produce, consume):
    pending = None
    for r in range(n_blocks):
        current = produce(r)
        if pending is not None:
            consume(r - 1, pending)
        pending = current
    consume(n_blocks - 1, pending)


def _layer_spec(layer, rows, cols):
    return pl.BlockSpec((None, rows, cols), lambda *_: (layer, 0, 0), pipeline_mode=pl.Buffered(1))


def _mod_specs(layer, tiles_per_batch, d):
    def spec(k):
        return pl.BlockSpec((None, None, None, 1, d),
                            lambda m, *_: (layer, m // tiles_per_batch, k, 0, 0))
    return spec(0), spec(1), spec(2)


def _mod_kernel(c_ref, w_ref, b_ref, o_ref):
    cond = _silu(c_ref[...]).astype(BF16)
    o_ref[0] = jnp.dot(cond, w_ref[0].astype(BF16), preferred_element_type=F32) + b_ref[0]


def _adaln_mod(c, ada_w, ada_b):
    depth, d, n = ada_w.shape
    b = c.shape[0]
    tn = _tile(n, MOD_COL_TILE, LANES)
    return pl.pallas_call(
        _mod_kernel,
        grid=(depth, n // tn),
        in_specs=[
            pl.BlockSpec((b, d), lambda i, j: (0, 0)),
            pl.BlockSpec((1, d, tn), lambda i, j: (i, 0, j)),
            pl.BlockSpec((1, 1, tn), lambda i, j: (i, 0, j)),
        ],
        out_specs=pl.BlockSpec((1, b, tn), lambda i, j: (i, 0, j)),
        out_shape=jax.ShapeDtypeStruct((depth, b, n), F32),
        compiler_params=_params("arbitrary", "arbitrary"),
        name="adaln_mod",
    )(c, ada_w, ada_b.reshape(depth, 1, n))


def _sgu_v_kernel(x_ref, g_ref, shift_ref, scale_ref, w_ref, h_ref, vg_ref, mu_ref, rstd_ref):
    tm = x_ref.shape[0]
    tr = min(ROW_BLOCK, tm)

    def project(r):
        rows = slice(r * tr, (r + 1) * tr)
        h = _modulated_norm(x_ref[rows, :], g_ref[...], shift_ref[...], scale_ref[...]).astype(BF16)
        h_ref[rows, :] = h
        return jnp.dot(h, w_ref[...], preferred_element_type=F32)

    def activate(r, acc):
        rows = slice(r * tr, (r + 1) * tr)
        v = _gelu(acc)
        vg_ref[rows, :] = v.astype(BF16)
        mu = jnp.mean(v, axis=-1, keepdims=True)
        var = jnp.mean(v * v, axis=-1, keepdims=True) - mu * mu
        mu_ref[rows, :] = mu
        rstd_ref[rows, :] = lax.rsqrt(var + NORM_EPS)

    _pipelined(tm // tr, project, activate)


def _sgu_v(x, pre_g, mod, w_in, layer, mixer_layer, seq):
    m, d = x.shape
    e = w_in.shape[2] // 3
    tm = _tile(seq, TOKEN_TILE, SGU_BLOCK)
    row = pl.BlockSpec((tm, d), lambda i: (i, 0))
    stat = pl.BlockSpec((tm, 1), lambda i: (i, 0))
    shift_spec, scale_spec, _ = _mod_specs(layer, seq // tm, d)
    return pl.pallas_call(
        _sgu_v_kernel,
        grid=(m // tm,),
        in_specs=[
            row, _layer_spec(layer, 1, d), shift_spec, scale_spec,
            pl.BlockSpec((None, d, e), lambda i: (mixer_layer, 0, 1), pipeline_mode=pl.Buffered(1)),
        ],
        out_specs=[row, pl.BlockSpec((tm, e), lambda i: (i, 0)), stat, stat],
        out_shape=[
            jax.ShapeDtypeStruct((m, d), BF16),
            jax.ShapeDtypeStruct((m, e), BF16),
            jax.ShapeDtypeStruct((m, 1), F32),
            jax.ShapeDtypeStruct((m, 1), F32),
        ],
        compiler_params=_params("arbitrary"),
        name="sgu_v",
    )(x, pre_g, mod, mod, w_in)


def _sgu_gate_kernel(h_ref, wu_ref, wz_ref, vg_ref, mu_ref, rstd_ref, ng_ref, ws_ref, bs_ref,
                     y_ref, *, groups_per_step):
    tm = h_ref.shape[0]
    tr = min(ROW_BLOCK, tm)
    nblk = tr // SGU_BLOCK
    t_chunk = lax.broadcasted_iota(jnp.int32, (SGU_BLOCK, SGU_BLOCK), 0) // CHUNK
    s_chunk = lax.broadcasted_iota(jnp.int32, (SGU_BLOCK, SGU_BLOCK), 1) // CHUNK
    causal = s_chunk <= t_chunk
    ws = [jnp.where(causal, ws_ref[gi], 0.0).astype(BF16) for gi in range(groups_per_step)]

    def project(r):
        rows = slice(r * tr, (r + 1) * tr)
        h = h_ref[rows, :]
        vn = ((vg_ref[rows, :].astype(F32) - mu_ref[rows, :]) * rstd_ref[rows, :]
              * ng_ref[...]).astype(BF16)
        vms = []
        for gi in range(groups_per_step):
            cols = slice(gi * SGU_GROUP_DIM, (gi + 1) * SGU_GROUP_DIM)
            vcat = jnp.concatenate(
                [vn[b * SGU_BLOCK:(b + 1) * SGU_BLOCK, cols] for b in range(nblk)], axis=1)
            vms.append(jnp.dot(ws[gi], vcat, preferred_element_type=F32))
        return (jnp.dot(h, wu_ref[...], preferred_element_type=F32),
                jnp.dot(h, wz_ref[...], preferred_element_type=F32), vms)

    def gate(r, projected):
        u_acc, z_acc, vms = projected
        u = _gelu(u_acc)
        z = _silu(z_acc)
        for gi in range(groups_per_step):
            cols = slice(gi * SGU_GROUP_DIM, (gi + 1) * SGU_GROUP_DIM)
            vm = vms[gi] + bs_ref[gi]
            for b in range(nblk):
                rows = slice(b * SGU_BLOCK, (b + 1) * SGU_BLOCK)
                vmb = vm[:, b * SGU_GROUP_DIM:(b + 1) * SGU_GROUP_DIM]
                y_ref[r * tr + b * SGU_BLOCK:r * tr + (b + 1) * SGU_BLOCK, cols] = (
                    u[rows, cols] * vmb * z[rows, cols]).astype(BF16)

    _pipelined(tm // tr, project, gate)


def _sgu_gate(h, w_in, vg, mu, rstd, norm_g, w_s, b_s, layer, seq):
    m, d = h.shape
    e = w_in.shape[2] // 3
    groups = e // SGU_GROUP_DIM
    gs = next(g for g in (4, 2, 1) if groups % g == 0)
    tn = gs * SGU_GROUP_DIM
    tm = _tile(seq, SGU_GATE_TOKEN_TILE, SGU_BLOCK)
    return pl.pallas_call(
        functools.partial(_sgu_gate_kernel, groups_per_step=gs),
        grid=(m // tm, groups // gs),
        in_specs=[
            pl.BlockSpec((tm, d), lambda i, j: (i, 0)),
            pl.BlockSpec((None, d, tn), lambda i, j: (layer, 0, j)),
            pl.BlockSpec((None, d, tn), lambda i, j: (layer, 0, 2 * (e // tn) + j)),
            pl.BlockSpec((tm, tn), lambda i, j: (i, j)),
            pl.BlockSpec((tm, 1), lambda i, j: (i, 0)),
            pl.BlockSpec((tm, 1), lambda i, j: (i, 0)),
            pl.BlockSpec((None, 1, tn), lambda i, j: (layer, 0, j)),
            pl.BlockSpec((None, gs, SGU_BLOCK, SGU_BLOCK), lambda i, j: (layer, j, 0, 0)),
            pl.BlockSpec((None, gs, SGU_BLOCK, 1), lambda i, j: (layer, j, 0, 0)),
        ],
        out_specs=pl.BlockSpec((tm, tn), lambda i, j: (i, j)),
        out_shape=jax.ShapeDtypeStruct((m, e), BF16),
        compiler_params=_params("arbitrary", "arbitrary"),
        name="sgu_gate",
    )(h, w_in, w_in, vg, mu, rstd, norm_g, w_s, b_s)


def _out_proj_kernel(y_ref, w_ref, x_ref, g_ref, gate_ref, o_ref, *, y_transposed):
    tm = o_ref.shape[0]
    tr = min(ROW_BLOCK, tm)

    def project(r):
        if y_transposed:
            return lax.dot_general(y_ref[:, r * tr:(r + 1) * tr], w_ref[...], _TN_DIMS,
                                   preferred_element_type=F32)
        return jnp.dot(y_ref[r * tr:(r + 1) * tr, :], w_ref[...], preferred_element_type=F32)

    def residual(r, acc):
        rows = slice(r * tr, (r + 1) * tr)
        ms = jnp.mean(acc * acc, axis=-1, keepdims=True)
        yn = acc * lax.rsqrt(ms + NORM_EPS) * g_ref[...]
        o_ref[rows, :] = x_ref[rows, :] + gate_ref[...] * yn

    _pipelined(tm // tr, project, residual)


def _out_proj(y, w_out, mixer_layer, x, post_g, mod, layer, seq):
    m, d = x.shape
    k = w_out.shape[1]
    if y.ndim == 2:
        tm = _tile(seq, TOKEN_TILE)
        y_spec = pl.BlockSpec((tm, k), lambda i: (i, 0))
    else:
        tm = y.shape[4]
        nq = seq // tm

        def pair_major(i):
            j = i % nq
            lower = j < nq // 2
            return (i // nq, jnp.where(lower, j, nq - 1 - j), jnp.where(lower, 0, 1), 0, 0)

        y_spec = pl.BlockSpec((None, None, None, k, tm), pair_major)
    row = pl.BlockSpec((tm, d), lambda i: (i, 0))
    _, _, gate_spec = _mod_specs(layer, seq // tm, d)
    return pl.pallas_call(
        functools.partial(_out_proj_kernel, y_transposed=y.ndim != 2),
        grid=(m // tm,),
        in_specs=[y_spec, _layer_spec(mixer_layer, k, d), row, _layer_spec(layer, 1, d), gate_spec],
        out_specs=row,
        out_shape=jax.ShapeDtypeStruct((m, d), F32),
        compiler_params=_params("arbitrary"),
        name="out_proj",
    )(y, w_out, x, post_g, mod)


def _mla_in_kernel(x_ref, g_ref, shift_ref, scale_ref, w_ref, wzt_ref, gq_ref, gkv_ref, cos_ref,
                   sin_ref, cq_ref, ckv_ref, kr_ref, szt_ref):
    h = _modulated_norm(x_ref[...], g_ref[...], shift_ref[...], scale_ref[...]).astype(BF16)
    lat = jnp.dot(h, w_ref[...], preferred_element_type=F32)
    cq = lat[:, :LATENT_PAD]
    ckv = lat[:, LATENT_PAD:2 * LATENT_PAD]
    kr = lat[:, 2 * LATENT_PAD:]
    ms_q = jnp.sum(cq * cq, axis=-1, keepdims=True) * (1.0 / Q_LORA_RANK)
    cq_ref[...] = (cq * lax.rsqrt(ms_q + NORM_EPS) * gq_ref[...]).astype(BF16)
    ms_kv = jnp.mean(ckv * ckv, axis=-1, keepdims=True)
    ckv_ref[...] = (ckv * lax.rsqrt(ms_kv + NORM_EPS) * gkv_ref[...]).astype(BF16)
    kr_ref[...] = _rope_rotate(kr, cos_ref[...], sin_ref[...]).astype(BF16)
    zt = lax.dot_general(wzt_ref[...], h, _NT_DIMS, preferred_element_type=F32)
    szt_ref[...] = _silu(zt).astype(BF16)


def _mla_in(x, pre_g, mod, w_lat, wzt, gq, gkv, cos_t, sin_t, layer, mixer_layer, seq):
    m, d = x.shape
    nlat = w_lat.shape[2]
    width = wzt.shape[1]
    tm = _tile(seq, TOKEN_TILE)
    tpb = seq // tm
    row = lambda n: pl.BlockSpec((tm, n), lambda i: (i, 0))
    table = pl.BlockSpec((tm, LANES), lambda i: (i % tpb, 0))
    shift_spec, scale_spec, _ = _mod_specs(layer, tpb, d)
    j = mixer_layer
    return pl.pallas_call(
        _mla_in_kernel,
        grid=(m // tm,),
        in_specs=[
            row(d), _layer_spec(layer, 1, d), shift_spec, scale_spec,
            _layer_spec(j, d, nlat), _layer_spec(j, width, d),
            _layer_spec(j, 1, LATENT_PAD), _layer_spec(j, 1, LATENT_PAD), table, table,
        ],
        out_specs=[row(LATENT_PAD), row(LATENT_PAD), row(LANES),
                   pl.BlockSpec((width, tm), lambda i: (0, i))],
        out_shape=[
            jax.ShapeDtypeStruct((m, LATENT_PAD), BF16),
            jax.ShapeDtypeStruct((m, LATENT_PAD), BF16),
            jax.ShapeDtypeStruct((m, LANES), BF16),
            jax.ShapeDtypeStruct((width, m), BF16),
        ],
        compiler_params=_params("arbitrary"),
        name="mla_in",
    )(x, pre_g, mod, mod, w_lat, wzt, gq, gkv, cos_t, sin_t)


def _mla_up_kernel(cq_ref, ckv_ref, kr_ref, wqt_ref, wk_ref, wvt_ref, cos_ref, sin_ref,
                   qt_ref, k_ref, vt_ref, *, heads, scale, tk):
    ckv = ckv_ref[...]
    qt = lax.dot_general(wqt_ref[...], cq_ref[...], _NT_DIMS, preferred_element_type=F32) * scale
    cos_t = cos_ref[...]
    sin_t = sin_ref[...]
    half = QK_ROPE_DIM // 2
    for hd in range(heads):
        r0 = hd * Q_HEAD_ROWS
        r1 = r0 + QK_NOPE_DIM
        qt_ref[r0:r1, :] = qt[r0:r1].astype(BF16)
        x1 = qt[r1:r1 + half]
        x2 = qt[r1 + half:r1 + QK_ROPE_DIM]
        qt_ref[r1:r1 + half, :] = (x1 * cos_t - x2 * sin_t).astype(BF16)
        qt_ref[r1 + half:r1 + QK_ROPE_DIM, :] = (x1 * sin_t + x2 * cos_t).astype(BF16)
    kn = jnp.dot(ckv, wk_ref[...], preferred_element_type=F32)
    kr = kr_ref[...]
    for hd in range(heads):
        c0 = hd * HEAD_PAD
        k_ref[:, c0:c0 + LANES] = kn[:, hd * QK_NOPE_DIM:(hd + 1) * QK_NOPE_DIM].astype(BF16)
        k_ref[:, c0 + LANES:c0 + HEAD_PAD] = kr
    vt = lax.dot_general(wvt_ref[...], ckv, _NT_DIMS, preferred_element_type=F32)
    for cb in range(vt.shape[1] // tk):
        vt_ref[cb] = vt[:, cb * tk:(cb + 1) * tk].astype(BF16)


def _mla_up(cq, ckv, kr, wqt, wk, wvt, cos_tt, sin_tt, layer, seq, heads, tk):
    m = cq.shape[0]
    tm = _tile(seq, TOKEN_TILE, tk)
    tpb = seq // tm
    row = lambda n: pl.BlockSpec((tm, n), lambda i: (i, 0))
    whole = lambda a: _layer_spec(layer, a.shape[1], a.shape[2])
    table = pl.BlockSpec((QK_ROPE_DIM // 2, tm), lambda i: (0, i % tpb))
    scale = float((QK_NOPE_DIM + QK_ROPE_DIM) ** -0.5 * 1.4426950408889634)
    return pl.pallas_call(
        functools.partial(_mla_up_kernel, heads=heads, scale=scale, tk=tk),
        grid=(m // tm,),
        in_specs=[row(LATENT_PAD), row(LATENT_PAD), row(LANES), whole(wqt), whole(wk), whole(wvt),
                  table, table],
        out_specs=[
            pl.BlockSpec((heads * Q_HEAD_ROWS, tm), lambda i: (0, i)),
            row(heads * HEAD_PAD),
            pl.BlockSpec((tm // tk, heads * V_HEAD_DIM, tk), lambda i: (i, 0, 0)),
        ],
        out_shape=[
            jax.ShapeDtypeStruct((heads * Q_HEAD_ROWS, m), BF16),
            jax.ShapeDtypeStruct((m, heads * HEAD_PAD), BF16),
            jax.ShapeDtypeStruct((m // tk, heads * V_HEAD_DIM, tk), BF16),
        ],
        compiler_params=_params("arbitrary"),
        name="mla_up",
    )(cq, ckv, kr, wqt, wk, wvt, cos_tt, sin_tt)


def _attn_kernel(qa_ref, qb_ref, k_ref, vt_ref, sza_ref, szb_ref, o_ref, m_sc, acc_sc,
                 *, tk, nq, heads):
    i = pl.program_id(2)
    tq = 2 * tk
    q_refs = (qa_ref, qb_ref)
    sz_refs = (sza_ref, szb_ref)

    def masked(s):
        visible = (lax.broadcasted_iota(jnp.int32, (tk, tk), 0) // CHUNK
                   <= lax.broadcasted_iota(jnp.int32, (tk, tk), 1) // CHUNK)
        return jnp.where(visible, s, -1e30)

    def score_stage(tile):
        hd, slot, t, nt, lane0, mask = tile
        k = k_ref[t * tk:(t + nt) * tk, hd * HEAD_PAD:(hd + 1) * HEAD_PAD]
        q = q_refs[slot][hd * Q_HEAD_ROWS:(hd + 1) * Q_HEAD_ROWS, lane0:]
        qt = jnp.concatenate([q[:QK_NOPE_DIM], jnp.zeros((ROPE_LANE0, q.shape[1]), BF16),
                              q[QK_NOPE_DIM:]], axis=0)
        s = jnp.dot(k, qt, preferred_element_type=F32)
        if mask == "left":
            s = jnp.concatenate([masked(s[:, :tk]), s[:, tk:]], axis=1)
        elif mask == "all":
            s = masked(s)
        return s, jnp.max(s, axis=0, keepdims=True)

    def exp_stage(tile, s, s_max):
        hd, slot, _, _, lane0, mask = tile
        state = 2 * hd + slot
        if mask == "left":
            m_sc[state] = s_max
            return jnp.exp2(s - s_max).astype(BF16), None
        lanes = pl.ds(lane0, tq - lane0)
        m_prev = m_sc[state, :, lanes]
        m_new = jnp.maximum(m_prev, s_max)
        m_sc[state, :, lanes] = m_new
        return jnp.exp2(s - m_new).astype(BF16), jnp.exp2(m_prev - m_new)

    def value_stage(tile, p, alpha):
        hd, slot, t, nt, lane0, _ = tile
        state = 2 * hd + slot
        rows = slice(hd * V_HEAD_DIM, (hd + 1) * V_HEAD_DIM)
        v_t = jnp.concatenate([vt_ref[t + u, rows, :] for u in range(nt)], axis=1)
        v_aug = jnp.concatenate([v_t, jnp.ones((ONES_ROWS, nt * tk), BF16)], axis=0)
        pv = jnp.dot(v_aug, p, preferred_element_type=F32)
        if alpha is None:
            acc_sc[state] = pv
        else:
            lanes = pl.ds(lane0, tq - lane0)
            acc_sc[state, :, lanes] = alpha * acc_sc[state, :, lanes] + pv

    def finalize(hd, slot):
        acc = acc_sc[2 * hd + slot]
        rows = slice(hd * V_HEAD_DIM, (hd + 1) * V_HEAD_DIM)
        out_t = acc[:V_HEAD_DIM] * (1.0 / acc[V_HEAD_DIM:V_HEAD_DIM + 1])
        o_ref[slot, rows, :] = (out_t * sz_refs[slot][rows, :].astype(F32)).astype(BF16)

    def run_pair(c):
        tiles = []
        last = {}
        for hd in range(heads):
            for slot, blk in ((1, nq - 1 - c), (0, c)):
                tiles.append((hd, slot, 2 * blk, 1, 0, "left"))
                tiles.append((hd, slot, 2 * blk + 1, 1, tk, "all"))
                tiles.extend((hd, slot, t, 1, 0, None) for t in range(2 * blk))
                last[len(tiles) - 1] = (hd, slot)
        scored = {}
        exped = {}
        for n in range(len(tiles) + 2):
            if n < len(tiles):
                scored[n] = score_stage(tiles[n])
            if 0 <= n - 1 < len(tiles):
                exped[n - 1] = exp_stage(tiles[n - 1], *scored.pop(n - 1))
            if 0 <= n - 2 < len(tiles):
                value_stage(tiles[n - 2], *exped.pop(n - 2))
                if n - 2 in last:
                    finalize(*last[n - 2])

    for c in range(nq // 2):
        pl.when(i == c)(functools.partial(run_pair, c))


def _attention(qt, k, vt, szt, batch, seq, heads, tk):
    tq = 2 * tk
    nq = seq // tq
    hps = ATTN_HEADS_PER_STEP if heads % ATTN_HEADS_PER_STEP == 0 else 1
    q_spec = lambda f: pl.BlockSpec((hps * Q_HEAD_ROWS, tq), lambda b, h, i: (h, b * nq + f(i)))
    sz_spec = lambda f: pl.BlockSpec((hps * V_HEAD_DIM, tq), lambda b, h, i: (h, b * nq + f(i)))
    first = lambda i: i
    second = lambda i: nq - 1 - i
    return pl.pallas_call(
        functools.partial(_attn_kernel, tk=tk, nq=nq, heads=hps),
        grid=(batch, heads // hps, nq // 2),
        in_specs=[
            q_spec(first), q_spec(second),
            pl.BlockSpec((seq, hps * HEAD_PAD), lambda b, h, i: (b, h)),
            pl.BlockSpec((seq // tk, hps * V_HEAD_DIM, tk), lambda b, h, i: (b, h, 0)),
            sz_spec(first), sz_spec(second),
        ],
        out_specs=pl.BlockSpec((None, None, 2, hps * V_HEAD_DIM, tq),
                               lambda b, h, i: (b, i, 0, h, 0)),
        out_shape=jax.ShapeDtypeStruct((batch, nq // 2, 2, heads * V_HEAD_DIM, tq), BF16),
        scratch_shapes=[
            pltpu.VMEM((2 * hps, 1, tq), F32),
            pltpu.VMEM((2 * hps, V_HEAD_DIM + ONES_ROWS, tq), F32),
        ],
        compiler_params=_params("arbitrary", "arbitrary", "arbitrary"),
        name="mla_attention",
    )(qt, qt, k, vt, szt, szt)


def _rope_tables(seq):
    pos = jnp.arange(seq, dtype=F32)
    inv_freq = ROPE_THETA ** (-jnp.arange(0, QK_ROPE_DIM, 2, dtype=F32) / QK_ROPE_DIM)
    ang = pos[:, None] * inv_freq[None, :]
    cos, sin = jnp.cos(ang), jnp.sin(ang)
    ones = jnp.ones((seq, ROPE_LANE0), F32)
    zeros = jnp.zeros((seq, ROPE_LANE0), F32)
    return (jnp.concatenate([ones, cos, cos], axis=-1),
            jnp.concatenate([zeros, -sin, sin], axis=-1), cos.T, sin.T)


def _mla_weight_layout(w_in, gq, w_uq, w_ukv, heads):
    nb, d, _ = w_in.shape
    o1, o2, o3 = Q_LORA_RANK, Q_LORA_RANK + KV_LORA_RANK, Q_LORA_RANK + KV_LORA_RANK + QK_ROPE_DIM
    zc = lambda n: jnp.zeros((nb, d, n), w_in.dtype)
    w_lat = jnp.concatenate([
        w_in[..., :o1], zc(LATENT_PAD - Q_LORA_RANK),
        w_in[..., o1:o2], zc(LATENT_PAD - KV_LORA_RANK),
        zc(ROPE_LANE0), w_in[..., o2:o3],
    ], axis=-1).astype(BF16)
    wzt = jnp.swapaxes(w_in[..., o3:], 1, 2).astype(BF16)
    gq_l = jnp.pad(gq, ((0, 0), (0, LATENT_PAD - Q_LORA_RANK)))[:, None, :]
    wqt = jnp.swapaxes(
        jnp.pad(w_uq, ((0, 0), (0, LATENT_PAD - Q_LORA_RANK), (0, 0))), 1, 2).astype(BF16)
    wkv = w_ukv.reshape(nb, KV_LORA_RANK, heads, QK_NOPE_DIM + V_HEAD_DIM)
    wk = wkv[..., :QK_NOPE_DIM].reshape(nb, KV_LORA_RANK, heads * QK_NOPE_DIM).astype(BF16)
    wvt = jnp.swapaxes(
        wkv[..., QK_NOPE_DIM:].reshape(nb, KV_LORA_RANK, heads * V_HEAD_DIM), 1, 2).astype(BF16)
    return w_lat, wzt, gq_l, wqt, wk, wvt


def kernel(x, c, ada_w, ada_b, pre_g, post_g, sgu_w_in, sgu_norm_g, sgu_w_s, sgu_b_s, sgu_w_out,
           mla_w_in, mla_q_norm_g, mla_kv_norm_g, mla_w_uq, mla_w_ukv, mla_w_out):
    batch, seq, d = x.shape
    depth = ada_w.shape[0]
    heads = mla_w_out.shape[1] // V_HEAD_DIM
    tk = ATTN_KV_TILE
    assert seq % SGU_BLOCK == 0 and seq % (2 * tk) == 0 and d % LANES == 0
    assert sgu_w_in.shape[2] % (3 * SGU_GROUP_DIM) == 0
    assert mla_kv_norm_g.shape[1] == KV_LORA_RANK == LATENT_PAD

    mod = _adaln_mod(c, ada_w, ada_b).reshape(depth, batch, 3, 1, d)
    pre_g3 = pre_g[:, None, :]
    post_g3 = post_g[:, None, :]
    sgu_w_in_l = sgu_w_in.astype(BF16)
    sgu_w_out_l = sgu_w_out.astype(BF16)
    sgu_norm_g3 = sgu_norm_g[:, None, :]
    sgu_b_s4 = sgu_b_s[..., None]
    w_lat, wzt, gq_l, wqt, wk, wvt = _mla_weight_layout(
        mla_w_in, mla_q_norm_g, mla_w_uq, mla_w_ukv, heads)
    gkv_l = mla_kv_norm_g[:, None, :]
    mla_w_out_l = mla_w_out.astype(BF16)
    cos_t, sin_t, cos_tt, sin_tt = _rope_tables(seq)

    xf = x.reshape(batch * seq, d)
    for i in range(depth):
        j = i // 2
        if i % 2 == 0:
            h, vg, mu, rstd = _sgu_v(xf, pre_g3, mod, sgu_w_in_l, i, j, seq)
            y = _sgu_gate(h, sgu_w_in_l, vg, mu, rstd, sgu_norm_g3, sgu_w_s, sgu_b_s4, j, seq)
            w_out = sgu_w_out_l
        else:
            cq, ckv, kr, szt = _mla_in(xf, pre_g3, mod, w_lat, wzt, gq_l, gkv_l, cos_t, sin_t,
                                       i, j, seq)
            qt, k, vt = _mla_up(cq, ckv, kr, wqt, wk, wvt, cos_tt, sin_tt, j, seq, heads, tk)
            y = _attention(qt, k, vt, szt, batch, seq, heads, tk)
            w_out = mla_w_out_l
        xf = _out_proj(y, w_out, j, xf, post_g3, mod, i, seq)
    return xf.reshape(batch, seq, d)
```

```python
import functools

import jax
import jax.numpy as jnp
from jax import lax
from jax.experimental import pallas as pl
from jax.experimental.pallas import tpu as pltpu

F32 = jnp.float32
BF16 = jnp.bfloat16

NORM_EPS = 1e-6
CHUNK = 64
SGU_BLOCK = 128
SGU_GROUP_DIM = 256
Q_LORA_RANK = 448
KV_LORA_RANK = 512
QK_NOPE_DIM = 128
QK_ROPE_DIM = 64
V_HEAD_DIM = 128
ROPE_THETA = 10000.0
LANES = 128
LATENT_PAD = 512
HEAD_PAD = 2 * LANES
Q_HEAD_ROWS = QK_NOPE_DIM + QK_ROPE_DIM
ROPE_LANE0 = LANES - QK_ROPE_DIM
ATTN_KV_TILE = 256
ATTN_HEADS_PER_STEP = 2
ONES_ROWS = 16
VMEM_LIMIT_BYTES = 60 * 1024 * 1024
_NN_DIMS = (((1,), (0,)), ((), ()))
_NT_DIMS = (((1,), (1,)), ((), ()))
_TN_DIMS = (((0,), (0,)), ((), ()))


def _tile(n, pref, mult=8):
    t = min(n, pref)
    while t > mult and (n % t or t % mult):
        t -= mult
    return t if n % t == 0 else n


def _params(*semantics):
    return pltpu.CompilerParams(dimension_semantics=semantics, vmem_limit_bytes=VMEM_LIMIT_BYTES)


def _gelu(x):
    return 0.5 * x * (1.0 + lax.erf(x * (0.5 ** 0.5)))


def _silu(x):
    hx = 0.5 * x
    return hx + hx * jnp.tanh(hx)


def _modulated_norm(x, g, shift, scale):
    ms = jnp.mean(x * x, axis=-1, keepdims=True)
    y = x * lax.rsqrt(ms + NORM_EPS) * g
    return y * (1.0 + scale) + shift


def _rope_rotate(x, cos_t, sin_t):
    lane = lax.broadcasted_iota(jnp.int32, x.shape, 1)
    half = QK_ROPE_DIM // 2
    partner = jnp.where(lane < ROPE_LANE0 + half,
                        pltpu.roll(x, LANES - half, 1),
                        pltpu.roll(x, half, 1))
    return x * cos_t + partner * sin_t


ROW_BLOCK = 256
TOKEN_TILE = 512
SGU_GATE_TOKEN_TILE = 1024
MOD_COL_TILE = 1536


def _pipelined(n_blocks, produce, consume):
    pending = None
    for r in range(n_blocks):
        current = produce(r)
        if pending is not None:
            consume(r - 1, pending)
        pending = current
    consume(n_blocks - 1, pending)


def _layer_spec(layer, rows, cols):
    return pl.BlockSpec((None, rows, cols), lambda *_: (layer, 0, 0), pipeline_mode=pl.Buffered(1))


def _mod_specs(layer, tiles_per_batch, d):
    def spec(k):
        return pl.BlockSpec((None, None, None, 1, d),
                            lambda m, *_: (layer, m // tiles_per_batch, k, 0, 0))
    return spec(0), spec(1), spec(2)


def _mod_kernel(c_ref, w_ref, b_ref, o_ref):
    cond = _silu(c_ref[...]).astype(BF16)
    o_ref[0] = jnp.dot(cond, w_ref[0].astype(BF16), preferred_element_type=F32) + b_ref[0]


def _adaln_mod(c, ada_w, ada_b):
    depth, d, n = ada_w.shape
    b = c.shape[0]
    tn = _tile(n, MOD_COL_TILE, LANES)
    return pl.pallas_call(
        _mod_kernel,
        grid=(depth, n // tn),
        in_specs=[
            pl.BlockSpec((b, d), lambda i, j: (0, 0)),
            pl.BlockSpec((1, d, tn), lambda i, j: (i, 0, j)),
            pl.BlockSpec((1, 1, tn), lambda i, j: (i, 0, j)),
        ],
        out_specs=pl.BlockSpec((1, b, tn), lambda i, j: (i, 0, j)),
        out_shape=jax.ShapeDtypeStruct((depth, b, n), F32),
        compiler_params=_params("arbitrary", "arbitrary"),
        name="adaln_mod",
    )(c, ada_w, ada_b.reshape(depth, 1, n))


def _sgu_v_kernel(x_ref, g_ref, shift_ref, scale_ref, w_ref, h_ref, vg_ref, mu_ref, rstd_ref):
    tm = x_ref.shape[0]
    tr = min(ROW_BLOCK, tm)

    def project(r):
        rows = slice(r * tr, (r + 1) * tr)
        h = _modulated_norm(x_ref[rows, :], g_ref[...], shift_ref[...], scale_ref[...]).astype(BF16)
        h_ref[rows, :] = h
        return jnp.dot(h, w_ref[...], preferred_element_type=F32)

    def activate(r, acc):
        rows = slice(r * tr, (r + 1) * tr)
        v = _gelu(acc)
        vg_ref[rows, :] = v.astype(BF16)
        mu = jnp.mean(v, axis=-1, keepdims=True)
        var = jnp.mean(v * v, axis=-1, keepdims=True) - mu * mu
        mu_ref[rows, :] = mu
        rstd_ref[rows, :] = lax.rsqrt(var + NORM_EPS)

    _pipelined(tm // tr, project, activate)


def _sgu_v(x, pre_g, mod, w_in, layer, mixer_layer, seq):
    m, d = x.shape
    e = w_in.shape[2] // 3
    tm = _tile(seq, TOKEN_TILE, SGU_BLOCK)
    row = pl.BlockSpec((tm, d), lambda i: (i, 0))
    stat = pl.BlockSpec((tm, 1), lambda i: (i, 0))
    shift_spec, scale_spec, _ = _mod_specs(layer, seq // tm, d)
    return pl.pallas_call(
        _sgu_v_kernel,
        grid=(m // tm,),
        in_specs=[
            row, _layer_spec(layer, 1, d), shift_spec, scale_spec,
            pl.BlockSpec((None, d, e), lambda i: (mixer_layer, 0, 1), pipeline_mode=pl.Buffered(1)),
        ],
        out_specs=[row, pl.BlockSpec((tm, e), lambda i: (i, 0)), stat, stat],
        out_shape=[
            jax.ShapeDtypeStruct((m, d), BF16),
            jax.ShapeDtypeStruct((m, e), BF16),
            jax.ShapeDtypeStruct((m, 1), F32),
            jax.ShapeDtypeStruct((m, 1), F32),
        ],
        compiler_params=_params("arbitrary"),
        name="sgu_v",
    )(x, pre_g, mod, mod, w_in)


def _sgu_gate_kernel(h_ref, wu_ref, wz_ref, vg_ref, mu_ref, rstd_ref, ng_ref, ws_ref, bs_ref,
                     y_ref, *, groups_per_step):
    tm = h_ref.shape[0]
    tr = min(ROW_BLOCK, tm)
    nblk = tr // SGU_BLOCK
    t_chunk = lax.broadcasted_iota(jnp.int32, (SGU_BLOCK, SGU_BLOCK), 0) // CHUNK
    s_chunk = lax.broadcasted_iota(jnp.int32, (SGU_BLOCK, SGU_BLOCK), 1) // CHUNK
    causal = s_chunk <= t_chunk
    ws = [jnp.where(causal, ws_ref[gi], 0.0).astype(BF16) for gi in range(groups_per_step)]

    def project(r):
        rows = slice(r * tr, (r + 1) * tr)
        h = h_ref[rows, :]
        vn = ((vg_ref[rows, :].astype(F32) - mu_ref[rows, :]) * rstd_ref[rows, :]
              * ng_ref[...]).astype(BF16)
        vms = []
        for gi in range(groups_per_step):
            cols = slice(gi * SGU_GROUP_DIM, (gi + 1) * SGU_GROUP_DIM)
            vcat = jnp.concatenate(
                [vn[b * SGU_BLOCK:(b + 1) * SGU_BLOCK, cols] for b in range(nblk)], axis=1)
            vms.append(jnp.dot(ws[gi], vcat, preferred_element_type=F32))
        return (jnp.dot(h, wu_ref[...], preferred_element_type=F32),
                jnp.dot(h, wz_ref[...], preferred_element_type=F32), vms)

    def gate(r, projected):
        u_acc, z_acc, vms = projected
        u = _gelu(u_acc)
        z = _silu(z_acc)
        for gi in range(groups_per_step):
            cols = slice(gi * SGU_GROUP_DIM, (gi + 1) * SGU_GROUP_DIM)
            vm = vms[gi] + bs_ref[gi]
            for b in range(nblk):
                rows = slice(b * SGU_BLOCK, (b + 1) * SGU_BLOCK)
                vmb = vm[:, b * SGU_GROUP_DIM:(b + 1) * SGU_GROUP_DIM]
                y_ref[r * tr + b * SGU_BLOCK:r * tr + (b + 1) * SGU_BLOCK, cols] = (
                    u[rows, cols] * vmb * z[rows, cols]).astype(BF16)

    _pipelined(tm // tr, project, gate)


def _sgu_gate(h, w_in, vg, mu, rstd, norm_g, w_s, b_s, layer, seq):
    m, d = h.shape
    e = w_in.shape[2] // 3
    groups = e // SGU_GROUP_DIM
    gs = next(g for g in (4, 2, 1) if groups % g == 0)
    tn = gs * SGU_GROUP_DIM
    tm = _tile(seq, SGU_GATE_TOKEN_TILE, SGU_BLOCK)
    return pl.pallas_call(
        functools.partial(_sgu_gate_kernel, groups_per_step=gs),
        grid=(m // tm, groups // gs),
        in_specs=[
            pl.BlockSpec((tm, d), lambda i, j: (i, 0)),
            pl.BlockSpec((None, d, tn), lambda i, j: (layer, 0, j)),
            pl.BlockSpec((None, d, tn), lambda i, j: (layer, 0, 2 * (e // tn) + j)),
            pl.BlockSpec((tm, tn), lambda i, j: (i, j)),
            pl.BlockSpec((tm, 1), lambda i, j: (i, 0)),
            pl.BlockSpec((tm, 1), lambda i, j: (i, 0)),
            pl.BlockSpec((None, 1, tn), lambda i, j: (layer, 0, j)),
            pl.BlockSpec((None, gs, SGU_BLOCK, SGU_BLOCK), lambda i, j: (layer, j, 0, 0)),
            pl.BlockSpec((None, gs, SGU_BLOCK, 1), lambda i, j: (layer, j, 0, 0)),
        ],
        out_specs=pl.BlockSpec((tm, tn), lambda i, j: (i, j)),
        out_shape=jax.ShapeDtypeStruct((m, e), BF16),
        compiler_params=_params("arbitrary", "arbitrary"),
        name="sgu_gate",
    )(h, w_in, w_in, vg, mu, rstd, norm_g, w_s, b_s)


def _out_proj_kernel(y_ref, w_ref, x_ref, g_ref, gate_ref, o_ref, *, y_transposed):
    tm = o_ref.shape[0]
    tr = min(ROW_BLOCK, tm)

    def project(r):
        if y_transposed:
            return lax.dot_general(y_ref[:, r * tr:(r + 1) * tr], w_ref[...], _TN_DIMS,
                                   preferred_element_type=F32)
        return jnp.dot(y_ref[r * tr:(r + 1) * tr, :], w_ref[...], preferred_element_type=F32)

    def residual(r, acc):
        rows = slice(r * tr, (r + 1) * tr)
        ms = jnp.mean(acc * acc, axis=-1, keepdims=True)
        yn = acc * lax.rsqrt(ms + NORM_EPS) * g_ref[...]
        o_ref[rows, :] = x_ref[rows, :] + gate_ref[...] * yn

    _pipelined(tm // tr, project, residual)


def _out_proj(y, w_out, mixer_layer, x, post_g, mod, layer, seq):
    m, d = x.shape
    k = w_out.shape[1]
    if y.ndim == 2:
        tm = _tile(seq, TOKEN_TILE)
        y_spec = pl.BlockSpec((tm, k), lambda i: (i, 0))
    else:
        tm = y.shape[4]
        nq = seq // tm

        def pair_major(i):
            j = i % nq
            lower = j < nq // 2
            return (i // nq, jnp.where(lower, j, nq - 1 - j), jnp.where(lower, 0, 1), 0, 0)

        y_spec = pl.BlockSpec((None, None, None, k, tm), pair_major)
    row = pl.BlockSpec((tm, d), lambda i: (i, 0))
    _, _, gate_spec = _mod_specs(layer, seq // tm, d)
    return pl.pallas_call(
        functools.partial(_out_proj_kernel, y_transposed=y.ndim != 2),
        grid=(m // tm,),
        in_specs=[y_spec, _layer_spec(mixer_layer, k, d), row, _layer_spec(layer, 1, d), gate_spec],
        out_specs=row,
        out_shape=jax.ShapeDtypeStruct((m, d), F32),
        compiler_params=_params("arbitrary"),
        name="out_proj",
    )(y, w_out, x, post_g, mod)


def _mla_in_kernel(x_ref, g_ref, shift_ref, scale_ref, w_ref, wzt_ref, gq_ref, gkv_ref, cos_ref,
                   sin_ref, cq_ref, ckv_ref, kr_ref, szt_ref):
    h = _modulated_norm(x_ref[...], g_ref[...], shift_ref[...], scale_ref[...]).astype(BF16)
    lat = jnp.dot(h, w_ref[...], preferred_element_type=F32)
    cq = lat[:, :LATENT_PAD]
    ckv = lat[:, LATENT_PAD:2 * LATENT_PAD]
    kr = lat[:, 2 * LATENT_PAD:]
    ms_q = jnp.sum(cq * cq, axis=-1, keepdims=True) * (1.0 / Q_LORA_RANK)
    cq_ref[...] = (cq * lax.rsqrt(ms_q + NORM_EPS) * gq_ref[...]).astype(BF16)
    ms_kv = jnp.mean(ckv * ckv, axis=-1, keepdims=True)
    ckv_ref[...] = (ckv * lax.rsqrt(ms_kv + NORM_EPS) * gkv_ref[...]).astype(BF16)
    kr_ref[...] = _rope_rotate(kr, cos_ref[...], sin_ref[...]).astype(BF16)
    zt = lax.dot_general(wzt_ref[...], h, _NT_DIMS, preferred_element_type=F32)
    szt_ref[...] = _silu(zt).astype(BF16)


def _mla_in(x, pre_g, mod, w_lat, wzt, gq, gkv, cos_t, sin_t, layer, mixer_layer, seq):
    m, d = x.shape
    nlat = w_lat.shape[2]
    width = wzt.shape[1]
    tm = _tile(seq, TOKEN_TILE)
    tpb = seq // tm
    row = lambda n: pl.BlockSpec((tm, n), lambda i: (i, 0))
    table = pl.BlockSpec((tm, LANES), lambda i: (i % tpb, 0))
    shift_spec, scale_spec, _ = _mod_specs(layer, tpb, d)
    j = mixer_layer
    return pl.pallas_call(
        _mla_in_kernel,
        grid=(m // tm,),
        in_specs=[
            row(d), _layer_spec(layer, 1, d), shift_spec, scale_spec,
            _layer_spec(j, d, nlat), _layer_spec(j, width, d),
            _layer_spec(j, 1, LATENT_PAD), _layer_spec(j, 1, LATENT_PAD), table, table,
        ],
        out_specs=[row(LATENT_PAD), row(LATENT_PAD), row(LANES),
                   pl.BlockSpec((width, tm), lambda i: (0, i))],
        out_shape=[
            jax.ShapeDtypeStruct((m, LATENT_PAD), BF16),
            jax.ShapeDtypeStruct((m, LATENT_PAD), BF16),
            jax.ShapeDtypeStruct((m, LANES), BF16),
            jax.ShapeDtypeStruct((width, m), BF16),
        ],
        compiler_params=_params("arbitrary"),
        name="mla_in",
    )(x, pre_g, mod, mod, w_lat, wzt, gq, gkv, cos_t, sin_t)


def _mla_up_kernel(cq_ref, ckv_ref, kr_ref, wqt_ref, wk_ref, wvt_ref, cos_ref, sin_ref,
                   qt_ref, k_ref, vt_ref, *, heads, scale, tk):
    ckv = ckv_ref[...]
    qt = lax.dot_general(wqt_ref[...], cq_ref[...], _NT_DIMS, preferred_element_type=F32) * scale
    cos_t = cos_ref[...]
    sin_t = sin_ref[...]
    half = QK_ROPE_DIM // 2
    for hd in range(heads):
        r0 = hd * Q_HEAD_ROWS
        r1 = r0 + QK_NOPE_DIM
        qt_ref[r0:r1, :] = qt[r0:r1].astype(BF16)
        x1 = qt[r1:r1 + half]
        x2 = qt[r1 + half:r1 + QK_ROPE_DIM]
        qt_ref[r1:r1 + half, :] = (x1 * cos_t - x2 * sin_t).astype(BF16)
        qt_ref[r1 + half:r1 + QK_ROPE_DIM, :] = (x1 * sin_t + x2 * cos_t).astype(BF16)
    kn = jnp.dot(ckv, wk_ref[...], preferred_element_type=F32)
    kr = kr_ref[...]
    for hd in range(heads):
        c0 = hd * HEAD_PAD
        k_ref[:, c0:c0 + LANES] = kn[:, hd * QK_NOPE_DIM:(hd + 1) * QK_NOPE_DIM].astype(BF16)
        k_ref[:, c0 + LANES:c0 + HEAD_PAD] = kr
    vt = lax.dot_general(wvt_ref[...], ckv, _NT_DIMS, preferred_element_type=F32)
    for cb in range(vt.shape[1] // tk):
        vt_ref[cb] = vt[:, cb * tk:(cb + 1) * tk].astype(BF16)


def _mla_up(cq, ckv, kr, wqt, wk, wvt, cos_tt, sin_tt, layer, seq, heads, tk):
    m = cq.shape[0]
    tm = _tile(seq, TOKEN_TILE, tk)
    tpb = seq // tm
    row = lambda n: pl.BlockSpec((tm, n), lambda i: (i, 0))
    whole = lambda a: _layer_spec(layer, a.shape[1], a.shape[2])
    table = pl.BlockSpec((QK_ROPE_DIM // 2, tm), lambda i: (0, i % tpb))
    scale = float((QK_NOPE_DIM + QK_ROPE_DIM) ** -0.5 * 1.4426950408889634)
    return pl.pallas_call(
        functools.partial(_mla_up_kernel, heads=heads, scale=scale, tk=tk),
        grid=(m // tm,),
        in_specs=[row(LATENT_PAD), row(LATENT_PAD), row(LANES), whole(wqt), whole(wk), whole(wvt),
                  table, table],
        out_specs=[
            pl.BlockSpec((heads * Q_HEAD_ROWS, tm), lambda i: (0, i)),
            row(heads * HEAD_PAD),
            pl.BlockSpec((tm // tk, heads * V_HEAD_DIM, tk), lambda i: (i, 0, 0)),
        ],
        out_shape=[
            jax.ShapeDtypeStruct((heads * Q_HEAD_ROWS, m), BF16),
            jax.ShapeDtypeStruct((m, heads * HEAD_PAD), BF16),
            jax.ShapeDtypeStruct((m // tk, heads * V_HEAD_DIM, tk), BF16),
        ],
        compiler_params=_params("arbitrary"),
        name="mla_up",
    )(cq, ckv, kr, wqt, wk, wvt, cos_tt, sin_tt)


def _attn_kernel(qa_ref, qb_ref, k_ref, vt_ref, sza_ref, szb_ref, o_ref, m_sc, acc_sc,
                 *, tk, nq, heads):
    i = pl.program_id(2)
    tq = 2 * tk
    q_refs = (qa_ref, qb_ref)
    sz_refs = (sza_ref, szb_ref)

    def masked(s):
        visible = (lax.broadcasted_iota(jnp.int32, (tk, tk), 0) // CHUNK
                   <= lax.broadcasted_iota(jnp.int32, (tk, tk), 1) // CHUNK)
        return jnp.where(visible, s, -1e30)

    def score_stage(tile):
        hd, slot, t, nt, lane0, mask = tile
        k = k_ref[t * tk:(t + nt) * tk, hd * HEAD_PAD:(hd + 1) * HEAD_PAD]
        q = q_refs[slot][hd * Q_HEAD_ROWS:(hd + 1) * Q_HEAD_ROWS, lane0:]
        qt = jnp.concatenate([q[:QK_NOPE_DIM], jnp.zeros((ROPE_LANE0, q.shape[1]), BF16),
                              q[QK_NOPE_DIM:]], axis=0)
        s = jnp.dot(k, qt, preferred_element_type=F32)
        if mask == "left":
            s = jnp.concatenate([masked(s[:, :tk]), s[:, tk:]], axis=1)
        elif mask == "all":
            s = masked(s)
        return s, jnp.max(s, axis=0, keepdims=True)

    def exp_stage(tile, s, s_max):
        hd, slot, _, _, lane0, mask = tile
        state = 2 * hd + slot
        if mask == "left":
            m_sc[state] = s_max
            return jnp.exp2(s - s_max).astype(BF16), None
        lanes = pl.ds(lane0, tq - lane0)
        m_prev = m_sc[state, :, lanes]
        m_new = jnp.maximum(m_prev, s_max)
        m_sc[state, :, lanes] = m_new
        return jnp.exp2(s - m_new).astype(BF16), jnp.exp2(m_prev - m_new)

    def value_stage(tile, p, alpha):
        hd, slot, t, nt, lane0, _ = tile
        state = 2 * hd + slot
        rows = slice(hd * V_HEAD_DIM, (hd + 1) * V_HEAD_DIM)
        v_t = jnp.concatenate([vt_ref[t + u, rows, :] for u in range(nt)], axis=1)
        v_aug = jnp.concatenate([v_t, jnp.ones((ONES_ROWS, nt * tk), BF16)], axis=0)
        pv = jnp.dot(v_aug, p, preferred_element_type=F32)
        if alpha is None:
            acc_sc[state] = pv
        else:
            lanes = pl.ds(lane0, tq - lane0)
            acc_sc[state, :, lanes] = alpha * acc_sc[state, :, lanes] + pv

    def finalize(hd, slot):
        acc = acc_sc[2 * hd + slot]
        rows = slice(hd * V_HEAD_DIM, (hd + 1) * V_HEAD_DIM)
        out_t = acc[:V_HEAD_DIM] * (1.0 / acc[V_HEAD_DIM:V_HEAD_DIM + 1])
        o_ref[slot, rows, :] = (out_t * sz_refs[slot][rows, :].astype(F32)).astype(BF16)

    def run_pair(c):
        tiles = []
        last = {}
        for hd in range(heads):
            for slot, blk in ((1, nq - 1 - c), (0, c)):
                tiles.append((hd, slot, 2 * blk, 1, 0, "left"))
                tiles.append((hd, slot, 2 * blk + 1, 1, tk, "all"))
                tiles.extend((hd, slot, t, 1, 0, None) for t in range(2 * blk))
                last[len(tiles) - 1] = (hd, slot)
        scored = {}
        exped = {}
        for n in range(len(tiles) + 2):
            if n < len(tiles):
                scored[n] = score_stage(tiles[n])
            if 0 <= n - 1 < len(tiles):
                exped[n - 1] = exp_stage(tiles[n - 1], *scored.pop(n - 1))
            if 0 <= n - 2 < len(tiles):
                value_stage(tiles[n - 2], *exped.pop(n - 2))
                if n - 2 in last:
                    finalize(*last[n - 2])

    for c in range(nq // 2):
        pl.when(i == c)(functools.partial(run_pair, c))


def _attention(qt, k, vt, szt, batch, seq, heads, tk):
    tq = 2 * tk
    nq = seq // tq
    hps = ATTN_HEADS_PER_STEP if heads % ATTN_HEADS_PER_STEP == 0 else 1
    q_spec = lambda f: pl.BlockSpec((hps * Q_HEAD_ROWS, tq), lambda b, h, i: (h, b * nq + f(i)))
    sz_spec = lambda f: pl.BlockSpec((hps * V_HEAD_DIM, tq), lambda b, h, i: (h, b * nq + f(i)))
    first = lambda i: i
    second = lambda i: nq - 1 - i
    return pl.pallas_call(
        functools.partial(_attn_kernel, tk=tk, nq=nq, heads=hps),
        grid=(batch, heads // hps, nq // 2),
        in_specs=[
            q_spec(first), q_spec(second),
            pl.BlockSpec((seq, hps * HEAD_PAD), lambda b, h, i: (b, h)),
            pl.BlockSpec((seq // tk, hps * V_HEAD_DIM, tk), lambda b, h, i: (b, h, 0)),
            sz_spec(first), sz_spec(second),
        ],
        out_specs=pl.BlockSpec((None, None, 2, hps * V_HEAD_DIM, tq),
                               lambda b, h, i: (b, i, 0, h, 0)),
        out_shape=jax.ShapeDtypeStruct((batch, nq // 2, 2, heads * V_HEAD_DIM, tq), BF16),
        scratch_shapes=[
            pltpu.VMEM((2 * hps, 1, tq), F32),
            pltpu.VMEM((2 * hps, V_HEAD_DIM + ONES_ROWS, tq), F32),
        ],
        compiler_params=_params("arbitrary", "arbitrary", "arbitrary"),
        name="mla_attention",
    )(qt, qt, k, vt, szt, szt)


def _rope_tables(seq):
    pos = jnp.arange(seq, dtype=F32)
    inv_freq = ROPE_THETA ** (-jnp.arange(0, QK_ROPE_DIM, 2, dtype=F32) / QK_ROPE_DIM)
    ang = pos[:, None] * inv_freq[None, :]
    cos, sin = jnp.cos(ang), jnp.sin(ang)
    ones = jnp.ones((seq, ROPE_LANE0), F32)
    zeros = jnp.zeros((seq, ROPE_LANE0), F32)
    return (jnp.concatenate([ones, cos, cos], axis=-1),
            jnp.concatenate([zeros, -sin, sin], axis=-1), cos.T, sin.T)


def _mla_weight_layout(w_in, gq, w_uq, w_ukv, heads):
    nb, d, _ = w_in.shape
    o1, o2, o3 = Q_LORA_RANK, Q_LORA_RANK + KV_LORA_RANK, Q_LORA_RANK + KV_LORA_RANK + QK_ROPE_DIM
    zc = lambda n: jnp.zeros((nb, d, n), w_in.dtype)
    w_lat = jnp.concatenate([
        w_in[..., :o1], zc(LATENT_PAD - Q_LORA_RANK),
        w_in[..., o1:o2], zc(LATENT_PAD - KV_LORA_RANK),
        zc(ROPE_LANE0), w_in[..., o2:o3],
    ], axis=-1).astype(BF16)
    wzt = jnp.swapaxes(w_in[..., o3:], 1, 2).astype(BF16)
    gq_l = jnp.pad(gq, ((0, 0), (0, LATENT_PAD - Q_LORA_RANK)))[:, None, :]
    wqt = jnp.swapaxes(
        jnp.pad(w_uq, ((0, 0), (0, LATENT_PAD - Q_LORA_RANK), (0, 0))), 1, 2).astype(BF16)
    wkv = w_ukv.reshape(nb, KV_LORA_RANK, heads, QK_NOPE_DIM + V_HEAD_DIM)
    wk = wkv[..., :QK_NOPE_DIM].reshape(nb, KV_LORA_RANK, heads * QK_NOPE_DIM).astype(BF16)
    wvt = jnp.swapaxes(
        wkv[..., QK_NOPE_DIM:].reshape(nb, KV_LORA_RANK, heads * V_HEAD_DIM), 1, 2).astype(BF16)
    return w_lat, wzt, gq_l, wqt, wk, wvt


def kernel(x, c, ada_w, ada_b, pre_g, post_g, sgu_w_in, sgu_norm_g, sgu_w_s, sgu_b_s, sgu_w_out,
           mla_w_in, mla_q_norm_g, mla_kv_norm_g, mla_w_uq, mla_w_ukv, mla_w_out):
    batch, seq, d = x.shape
    depth = ada_w.shape[0]
    heads = mla_w_out.shape[1] // V_HEAD_DIM
    tk = ATTN_KV_TILE
    assert seq % SGU_BLOCK == 0 and seq % (2 * tk) == 0 and d % LANES == 0
    assert sgu_w_in.shape[2] % (3 * SGU_GROUP_DIM) == 0
    assert mla_kv_norm_g.shape[1] == KV_LORA_RANK == LATENT_PAD

    mod = _adaln_mod(c, ada_w, ada_b).reshape(depth, batch, 3, 1, d)
    pre_g3 = pre_g[:, None, :]
    post_g3 = post_g[:, None, :]
    sgu_w_in_l = sgu_w_in.astype(BF16)
    sgu_w_out_l = sgu_w_out.astype(BF16)
    sgu_norm_g3 = sgu_norm_g[:, None, :]
    sgu_b_s4 = sgu_b_s[..., None]
    w_lat, wzt, gq_l, wqt, wk, wvt = _mla_weight_layout(
        mla_w_in, mla_q_norm_g, mla_w_uq, mla_w_ukv, heads)
    gkv_l = mla_kv_norm_g[:, None, :]
    mla_w_out_l = mla_w_out.astype(BF16)
    cos_t, sin_t, cos_tt, sin_tt = _rope_tables(seq)

    xf = x.reshape(batch * seq, d)
    for i in range(depth):
        j = i // 2
        if i % 2 == 0:
            h, vg, mu, rstd = _sgu_v(xf, pre_g3, mod, sgu_w_in_l, i, j, seq)
            y = _sgu_gate(h, sgu_w_in_l, vg, mu, rstd, sgu_norm_g3, sgu_w_s, sgu_b_s4, j, seq)
            w_out = sgu_w_out_l
        else:
            cq, ckv, kr, szt = _mla_in(xf, pre_g3, mod, w_lat, wzt, gq_l, gkv_l, cos_t, sin_t,
                                       i, j, seq)
            qt, k, vt = _mla_up(cq, ckv, kr, wqt, wk, wvt, cos_tt, sin_tt, j, seq, heads, tk)
            y = _attention(qt, k, vt, szt, batch, seq, heads, tk)
            w_out = mla_w_out_l
        xf = _out_proj(y, w_out, j, xf, post_g3, mod, i, seq)
    return xf.reshape(batch, seq, d)
```

```python
import functools

import jax
import jax.numpy as jnp
from jax import lax
from jax.experimental import pallas as pl
from jax.experimental.pallas import tpu as pltpu

F32 = jnp.float32
BF16 = jnp.bfloat16

NORM_EPS = 1e-6
CHUNK = 64
SGU_BLOCK = 128
SGU_GROUP_DIM = 256
Q_LORA_RANK = 448
KV_LORA_RANK = 512
QK_NOPE_DIM = 128
QK_ROPE_DIM = 64
V_HEAD_DIM = 128
ROPE_THETA = 10000.0
LANES = 128
LATENT_PAD = 512
HEAD_PAD = 2 * LANES
Q_HEAD_ROWS = QK_NOPE_DIM + QK_ROPE_DIM
ROPE_LANE0 = LANES - QK_ROPE_DIM
ATTN_KV_TILE = 256
ATTN_HEADS_PER_STEP = 2
ONES_ROWS = 16
VMEM_LIMIT_BYTES = 60 * 1024 * 1024
_NN_DIMS = (((1,), (0,)), ((), ()))
_NT_DIMS = (((1,), (1,)), ((), ()))
_TN_DIMS = (((0,), (0,)), ((), ()))


def _tile(n, pref, mult=8):
    t = min(n, pref)
    while t > mult and (n % t or t % mult):
        t -= mult
    return t if n % t == 0 else n


def _params(*semantics):
    return pltpu.CompilerParams(dimension_semantics=semantics, vmem_limit_bytes=VMEM_LIMIT_BYTES)


def _gelu(x):
    return 0.5 * x * (1.0 + lax.erf(x * (0.5 ** 0.5)))


def _silu(x):
    hx = 0.5 * x
    return hx + hx * jnp.tanh(hx)


def _modulated_norm(x, g, shift, scale):
    ms = jnp.mean(x * x, axis=-1, keepdims=True)
    y = x * lax.rsqrt(ms + NORM_EPS) * g
    return y * (1.0 + scale) + shift


def _rope_rotate(x, cos_t, sin_t):
    lane = lax.broadcasted_iota(jnp.int32, x.shape, 1)
    half = QK_ROPE_DIM // 2
    partner = jnp.where(lane < ROPE_LANE0 + half,
                        pltpu.roll(x, LANES - half, 1),
                        pltpu.roll(x, half, 1))
    return x * cos_t + partner * sin_t


ROW_BLOCK = 256
TOKEN_TILE = 512
SGU_GATE_TOKEN_TILE = 1024
MOD_COL_TILE = 1536


def _pipelined(n_blocks, produce, consume):
    pending = None
    for r in range(n_blocks):
        current = produce(r)
        if pending is not None:
            consume(r - 1, pending)
        pending = current
    consume(n_blocks - 1, pending)


def _layer_spec(layer, rows, cols):
    return pl.BlockSpec((None, rows, cols), lambda *_: (layer, 0, 0), pipeline_mode=pl.Buffered(1))


def _mod_specs(layer, tiles_per_batch, d):
    def spec(k):
        return pl.BlockSpec((None, None, None, 1, d),
                            lambda m, *_: (layer, m // tiles_per_batch, k, 0, 0))
    return spec(0), spec(1), spec(2)


def _mod_kernel(c_ref, w_ref, b_ref, o_ref):
    cond = _silu(c_ref[...]).astype(BF16)
    o_ref[0] = jnp.dot(cond, w_ref[0].astype(BF16), preferred_element_type=F32) + b_ref[0]


def _adaln_mod(c, ada_w, ada_b):
    depth, d, n = ada_w.shape
    b = c.shape[0]
    tn = _tile(n, MOD_COL_TILE, LANES)
    return pl.pallas_call(
        _mod_kernel,
        grid=(depth, n // tn),
        in_specs=[
            pl.BlockSpec((b, d), lambda i, j: (0, 0)),
            pl.BlockSpec((1, d, tn), lambda i, j: (i, 0, j)),
            pl.BlockSpec((1, 1, tn), lambda i, j: (i, 0, j)),
        ],
        out_specs=pl.BlockSpec((1, b, tn), lambda i, j: (i, 0, j)),
        out_shape=jax.ShapeDtypeStruct((depth, b, n), F32),
        compiler_params=_params("arbitrary", "arbitrary"),
        name="adaln_mod",
    )(c, ada_w, ada_b.reshape(depth, 1, n))


def _sgu_v_kernel(x_ref, g_ref, shift_ref, scale_ref, w_ref, h_ref, vg_ref, mu_ref, rstd_ref):
    tm = x_ref.shape[0]
    tr = min(ROW_BLOCK, tm)

    def project(r):
        rows = slice(r * tr, (r + 1) * tr)
        h = _modulated_norm(x_ref[rows, :], g_ref[...], shift_ref[...], scale_ref[...]).astype(BF16)
        h_ref[rows, :] = h
        return jnp.dot(h, w_ref[...], preferred_element_type=F32)

    def activate(r, acc):
        rows = slice(r * tr, (r + 1) * tr)
        v = _gelu(acc)
        vg_ref[rows, :] = v.astype(BF16)
        mu = jnp.mean(v, axis=-1, keepdims=True)
        var = jnp.mean(v * v, axis=-1, keepdims=True) - mu * mu
        mu_ref[rows, :] = mu
        rstd_ref[rows, :] = lax.rsqrt(var + NORM_EPS)

    _pipelined(tm // tr, project, activate)


def _sgu_v(x, pre_g, mod, w_in, layer, mixer_layer, seq):
    m, d = x.shape
    e = w_in.shape[2] // 3
    tm = _tile(seq, TOKEN_TILE, SGU_BLOCK)
    row = pl.BlockSpec((tm, d), lambda i: (i, 0))
    stat = pl.BlockSpec((tm, 1), lambda i: (i, 0))
    shift_spec, scale_spec, _ = _mod_specs(layer, seq // tm, d)
    return pl.pallas_call(
        _sgu_v_kernel,
        grid=(m // tm,),
        in_specs=[
            row, _layer_spec(layer, 1, d), shift_spec, scale_spec,
            pl.BlockSpec((None, d, e), lambda i: (mixer_layer, 0, 1), pipeline_mode=pl.Buffered(1)),
        ],
        out_specs=[row, pl.BlockSpec((tm, e), lambda i: (i, 0)), stat, stat],
        out_shape=[
            jax.ShapeDtypeStruct((m, d), BF16),
            jax.ShapeDtypeStruct((m, e), BF16),
            jax.ShapeDtypeStruct((m, 1), F32),
            jax.ShapeDtypeStruct((m, 1), F32),
        ],
        compiler_params=_params("arbitrary"),
        name="sgu_v",
    )(x, pre_g, mod, mod, w_in)


def _sgu_gate_kernel(h_ref, wu_ref, wz_ref, vg_ref, mu_ref, rstd_ref, ng_ref, ws_ref, bs_ref,
                     y_ref, *, groups_per_step):
    tm = h_ref.shape[0]
    tr = min(ROW_BLOCK, tm)
    nblk = tr // SGU_BLOCK
    t_chunk = lax.broadcasted_iota(jnp.int32, (SGU_BLOCK, SGU_BLOCK), 0) // CHUNK
    s_chunk = lax.broadcasted_iota(jnp.int32, (SGU_BLOCK, SGU_BLOCK), 1) // CHUNK
    causal = s_chunk <= t_chunk
    ws = [jnp.where(causal, ws_ref[gi], 0.0).astype(BF16) for gi in range(groups_per_step)]

    def project(r):
        rows = slice(r * tr, (r + 1) * tr)
        h = h_ref[rows, :]
        vn = ((vg_ref[rows, :].astype(F32) - mu_ref[rows, :]) * rstd_ref[rows, :]
              * ng_ref[...]).astype(BF16)
        vms = []
        for gi in range(groups_per_step):
            cols = slice(gi * SGU_GROUP_DIM, (gi + 1) * SGU_GROUP_DIM)
            vcat = jnp.concatenate(
                [vn[b * SGU_BLOCK:(b + 1) * SGU_BLOCK, cols] for b in range(nblk)], axis=1)
            vms.append(jnp.dot(ws[gi], vcat, preferred_element_type=F32))
        return (jnp.dot(h, wu_ref[...], preferred_element_type=F32),
                jnp.dot(h, wz_ref[...], preferred_element_type=F32), vms)

    def gate(r, projected):
        u_acc, z_acc, vms = projected
        u = _gelu(u_acc)
        z = _silu(z_acc)
        for gi in range(groups_per_step):
            cols = slice(gi * SGU_GROUP_DIM, (gi + 1) * SGU_GROUP_DIM)
            vm = vms[gi] + bs_ref[gi]
            for b in range(nblk):
                rows = slice(b * SGU_BLOCK, (b + 1) * SGU_BLOCK)
                vmb = vm[:, b * SGU_GROUP_DIM:(b + 1) * SGU_GROUP_DIM]
                y_ref[r * tr + b * SGU_BLOCK:r * tr + (b + 1) * SGU_BLOCK, cols] = (
                    u[rows, cols] * vmb * z[rows, cols]).astype(BF16)

    _pipelined(tm // tr, project, gate)


def _sgu_gate(h, w_in, vg, mu, rstd, norm_g, w_s, b_s, layer, seq):
    m, d = h.shape
    e = w_in.shape[2] // 3
    groups = e // SGU_GROUP_DIM
    gs = next(g for g in (4, 2, 1) if groups % g == 0)
    tn = gs * SGU_GROUP_DIM
    tm = _tile(seq, SGU_GATE_TOKEN_TILE, SGU_BLOCK)
    return pl.pallas_call(
        functools.partial(_sgu_gate_kernel, groups_per_step=gs),
        grid=(m // tm, groups // gs),
        in_specs=[
            pl.BlockSpec((tm, d), lambda i, j: (i, 0)),
            pl.BlockSpec((None, d, tn), lambda i, j: (layer, 0, j)),
            pl.BlockSpec((None, d, tn), lambda i, j: (layer, 0, 2 * (e // tn) + j)),
            pl.BlockSpec((tm, tn), lambda i, j: (i, j)),
            pl.BlockSpec((tm, 1), lambda i, j: (i, 0)),
            pl.BlockSpec((tm, 1), lambda i, j: (i, 0)),
            pl.BlockSpec((None, 1, tn), lambda i, j: (layer, 0, j)),
            pl.BlockSpec((None, gs, SGU_BLOCK, SGU_BLOCK), lambda i, j: (layer, j, 0, 0)),
            pl.BlockSpec((None, gs, SGU_BLOCK, 1), lambda i, j: (layer, j, 0, 0)),
        ],
        out_specs=pl.BlockSpec((tm, tn), lambda i, j: (i, j)),
        out_shape=jax.ShapeDtypeStruct((m, e), BF16),
        compiler_params=_params("arbitrary", "arbitrary"),
        name="sgu_gate",
    )(h, w_in, w_in, vg, mu, rstd, norm_g, w_s, b_s)


def _out_proj_kernel(y_ref, w_ref, x_ref, g_ref, gate_ref, o_ref, *, y_transposed):
    tm = o_ref.shape[0]
    tr = min(ROW_BLOCK, tm)

    def project(r):
        if y_transposed:
            return lax.dot_general(y_ref[:, r * tr:(r + 1) * tr], w_ref[...], _TN_DIMS,
                                   preferred_element_type=F32)
        return jnp.dot(y_ref[r * tr:(r + 1) * tr, :], w_ref[...], preferred_element_type=F32)

    def residual(r, acc):
        rows = slice(r * tr, (r + 1) * tr)
        ms = jnp.mean(acc * acc, axis=-1, keepdims=True)
        yn = acc * lax.rsqrt(ms + NORM_EPS) * g_ref[...]
        o_ref[rows, :] = x_ref[rows, :] + gate_ref[...] * yn

    _pipelined(tm // tr, project, residual)


def _out_proj(y, w_out, mixer_layer, x, post_g, mod, layer, seq):
    m, d = x.shape
    k = w_out.shape[1]
    if y.ndim == 2:
        tm = _tile(seq, TOKEN_TILE)
        y_spec = pl.BlockSpec((tm, k), lambda i: (i, 0))
    else:
        tm = y.shape[4]
        nq = seq // tm

        def pair_major(i):
            j = i % nq
            lower = j < nq // 2
            return (i // nq, jnp.where(lower, j, nq - 1 - j), jnp.where(lower, 0, 1), 0, 0)

        y_spec = pl.BlockSpec((None, None, None, k, tm), pair_major)
    row = pl.BlockSpec((tm, d), lambda i: (i, 0))
    _, _, gate_spec = _mod_specs(layer, seq // tm, d)
    return pl.pallas_call(
        functools.partial(_out_proj_kernel, y_transposed=y.ndim != 2),
        grid=(m // tm,),
        in_specs=[y_spec, _layer_spec(mixer_layer, k, d), row, _layer_spec(layer, 1, d), gate_spec],
        out_specs=row,
        out_shape=jax.ShapeDtypeStruct((m, d), F32),
        compiler_params=_params("arbitrary"),
        name="out_proj",
    )(y, w_out, x, post_g, mod)


def _mla_in_kernel(x_ref, g_ref, shift_ref, scale_ref, w_ref, wzt_ref, gq_ref, gkv_ref, cos_ref,
                   sin_ref, cq_ref, ckv_ref, kr_ref, szt_ref):
    h = _modulated_norm(x_ref[...], g_ref[...], shift_ref[...], scale_ref[...]).astype(BF16)
    lat = jnp.dot(h, w_ref[...], preferred_element_type=F32)
    cq = lat[:, :LATENT_PAD]
    ckv = lat[:, LATENT_PAD:2 * LATENT_PAD]
    kr = lat[:, 2 * LATENT_PAD:]
    ms_q = jnp.sum(cq * cq, axis=-1, keepdims=True) * (1.0 / Q_LORA_RANK)
    cq_ref[...] = (cq * lax.rsqrt(ms_q + NORM_EPS) * gq_ref[...]).astype(BF16)
    ms_kv = jnp.mean(ckv * ckv, axis=-1, keepdims=True)
    ckv_ref[...] = (ckv * lax.rsqrt(ms_kv + NORM_EPS) * gkv_ref[...]).astype(BF16)
    kr_ref[...] = _rope_rotate(kr, cos_ref[...], sin_ref[...]).astype(BF16)
    zt = lax.dot_general(wzt_ref[...], h, _NT_DIMS, preferred_element_type=F32)
    szt_ref[...] = _silu(zt).astype(BF16)


def _mla_in(x, pre_g, mod, w_lat, wzt, gq, gkv, cos_t, sin_t, layer, mixer_layer, seq):
    m, d = x.shape
    nlat = w_lat.shape[2]
    width = wzt.shape[1]
    tm = _tile(seq, TOKEN_TILE)
    tpb = seq // tm
    row = lambda n: pl.BlockSpec((tm, n), lambda i: (i, 0))
    table = pl.BlockSpec((tm, LANES), lambda i: (i % tpb, 0))
    shift_spec, scale_spec, _ = _mod_specs(layer, tpb, d)
    j = mixer_layer
    return pl.pallas_call(
        _mla_in_kernel,
        grid=(m // tm,),
        in_specs=[
            row(d), _layer_spec(layer, 1, d), shift_spec, scale_spec,
            _layer_spec(j, d, nlat), _layer_spec(j, width, d),
            _layer_spec(j, 1, LATENT_PAD), _layer_spec(j, 1, LATENT_PAD), table, table,
        ],
        out_specs=[row(LATENT_PAD), row(LATENT_PAD), row(LANES),
                   pl.BlockSpec((width, tm), lambda i: (0, i))],
        out_shape=[
            jax.ShapeDtypeStruct((m, LATENT_PAD), BF16),
            jax.ShapeDtypeStruct((m, LATENT_PAD), BF16),
            jax.ShapeDtypeStruct((m, LANES), BF16),
            jax.ShapeDtypeStruct((width, m), BF16),
        ],
        compiler_params=_params("arbitrary"),
        name="mla_in",
    )(x, pre_g, mod, mod, w_lat, wzt, gq, gkv, cos_t, sin_t)


def _mla_up_kernel(cq_ref, ckv_ref, kr_ref, wqt_ref, wk_ref, wvt_ref, cos_ref, sin_ref,
                   qt_ref, k_ref, vt_ref, *, heads, scale, tk):
    ckv = ckv_ref[...]
    qt = lax.dot_general(wqt_ref[...], cq_ref[...], _NT_DIMS, preferred_element_type=F32) * scale
    cos_t = cos_ref[...]
    sin_t = sin_ref[...]
    half = QK_ROPE_DIM // 2
    for hd in range(heads):
        r0 = hd * Q_HEAD_ROWS
        r1 = r0 + QK_NOPE_DIM
        qt_ref[r0:r1, :] = qt[r0:r1].astype(BF16)
        x1 = qt[r1:r1 + half]
        x2 = qt[r1 + half:r1 + QK_ROPE_DIM]
        qt_ref[r1:r1 + half, :] = (x1 * cos_t - x2 * sin_t).astype(BF16)
        qt_ref[r1 + half:r1 + QK_ROPE_DIM, :] = (x1 * sin_t + x2 * cos_t).astype(BF16)
    kn = jnp.dot(ckv, wk_ref[...], preferred_element_type=F32)
    kr = kr_ref[...]
    for hd in range(heads):
        c0 = hd * HEAD_PAD
        k_ref[:, c0:c0 + LANES] = kn[:, hd * QK_NOPE_DIM:(hd + 1) * QK_NOPE_DIM].astype(BF16)
        k_ref[:, c0 + LANES:c0 + HEAD_PAD] = kr
    vt = lax.dot_general(wvt_ref[...], ckv, _NT_DIMS, preferred_element_type=F32)
    for cb in range(vt.shape[1] // tk):
        vt_ref[cb] = vt[:, cb * tk:(cb + 1) * tk].astype(BF16)


def _mla_up(cq, ckv, kr, wqt, wk, wvt, cos_tt, sin_tt, layer, seq, heads, tk):
    m = cq.shape[0]
    tm = _tile(seq, TOKEN_TILE, tk)
    tpb = seq // tm
    row = lambda n: pl.BlockSpec((tm, n), lambda i: (i, 0))
    whole = lambda a: _layer_spec(layer, a.shape[1], a.shape[2])
    table = pl.BlockSpec((QK_ROPE_DIM // 2, tm), lambda i: (0, i % tpb))
    scale = float((QK_NOPE_DIM + QK_ROPE_DIM) ** -0.5 * 1.4426950408889634)
    return pl.pallas_call(
        functools.partial(_mla_up_kernel, heads=heads, scale=scale, tk=tk),
        grid=(m // tm,),
        in_specs=[row(LATENT_PAD), row(LATENT_PAD), row(LANES), whole(wqt), whole(wk), whole(wvt),
                  table, table],
        out_specs=[
            pl.BlockSpec((heads * Q_HEAD_ROWS, tm), lambda i: (0, i)),
            row(heads * HEAD_PAD),
            pl.BlockSpec((tm // tk, heads * V_HEAD_DIM, tk), lambda i: (i, 0, 0)),
        ],
        out_shape=[
            jax.ShapeDtypeStruct((heads * Q_HEAD_ROWS, m), BF16),
            jax.ShapeDtypeStruct((m, heads * HEAD_PAD), BF16),
            jax.ShapeDtypeStruct((m // tk, heads * V_HEAD_DIM, tk), BF16),
        ],
        compiler_params=_params("arbitrary"),
        name="mla_up",
    )(cq, ckv, kr, wqt, wk, wvt, cos_tt, sin_tt)


def _attn_kernel(qa_ref, qb_ref, k_ref, vt_ref, sza_ref, szb_ref, o_ref, m_sc, acc_sc,
                 *, tk, nq, heads):
    i = pl.program_id(2)
    tq = 2 * tk
    q_refs = (qa_ref, qb_ref)
    sz_refs = (sza_ref, szb_ref)

    def masked(s):
        visible = (lax.broadcasted_iota(jnp.int32, (tk, tk), 0) // CHUNK
                   <= lax.broadcasted_iota(jnp.int32, (tk, tk), 1) // CHUNK)
        return jnp.where(visible, s, -1e30)

    def score_stage(tile):
        hd, slot, t, nt, lane0, mask = tile
        k = k_ref[t * tk:(t + nt) * tk, hd * HEAD_PAD:(hd + 1) * HEAD_PAD]
        q = q_refs[slot][hd * Q_HEAD_ROWS:(hd + 1) * Q_HEAD_ROWS, lane0:]
        qt = jnp.concatenate([q[:QK_NOPE_DIM], jnp.zeros((ROPE_LANE0, q.shape[1]), BF16),
                              q[QK_NOPE_DIM:]], axis=0)
        s = jnp.dot(k, qt, preferred_element_type=F32)
        if mask == "left":
            s = jnp.concatenate([masked(s[:, :tk]), s[:, tk:]], axis=1)
        elif mask == "all":
            s = masked(s)
        return s, jnp.max(s, axis=0, keepdims=True)

    def exp_stage(tile, s, s_max):
        hd, slot, _, _, lane0, mask = tile
        state = 2 * hd + slot
        if mask == "left":
            m_sc[state] = s_max
            return jnp.exp2(s - s_max).astype(BF16), None
        lanes = pl.ds(lane0, tq - lane0)
        m_prev = m_sc[state, :, lanes]
        m_new = jnp.maximum(m_prev, s_max)
        m_sc[state, :, lanes] = m_new
        return jnp.exp2(s - m_new).astype(BF16), jnp.exp2(m_prev - m_new)

    def value_stage(tile, p, alpha):
        hd, slot, t, nt, lane0, _ = tile
        state = 2 * hd + slot
        rows = slice(hd * V_HEAD_DIM, (hd + 1) * V_HEAD_DIM)
        v_t = jnp.concatenate([vt_ref[t + u, rows, :] for u in range(nt)], axis=1)
        v_aug = jnp.concatenate([v_t, jnp.ones((ONES_ROWS, nt * tk), BF16)], axis=0)
        pv = jnp.dot(v_aug, p, preferred_element_type=F32)
        if alpha is None:
            acc_sc[state] = pv
        else:
            lanes = pl.ds(lane0, tq - lane0)
            acc_sc[state, :, lanes] = alpha * acc_sc[state, :, lanes] + pv

    def finalize(hd, slot):
        acc = acc_sc[2 * hd + slot]
        rows = slice(hd * V_HEAD_DIM, (hd + 1) * V_HEAD_DIM)
        out_t = acc[:V_HEAD_DIM] * (1.0 / acc[V_HEAD_DIM:V_HEAD_DIM + 1])
        o_ref[slot, rows, :] = (out_t * sz_refs[slot][rows, :].astype(F32)).astype(BF16)

    def run_pair(c):
        per_head = []
        for hd in range(heads):
            seq_tiles = []
            for slot, blk in ((1, nq - 1 - c), (0, c)):
                seq_tiles.append((hd, slot, 2 * blk, 1, 0, "left"))
                seq_tiles.append((hd, slot, 2 * blk + 1, 1, tk, "all"))
                seq_tiles.extend((hd, slot, t, 1, 0, None) for t in range(2 * blk))
            per_head.append(seq_tiles)
        tiles = [tile for group in zip(*per_head) for tile in group]
        last_of = {(t[0], t[1]): pos for pos, t in enumerate(tiles)}
        last = {pos: key for key, pos in last_of.items()}
        scored = {}
        exped = {}
        for n in range(len(tiles) + 2):
            if n < len(tiles):
                scored[n] = score_stage(tiles[n])
            if 0 <= n - 1 < len(tiles):
                exped[n - 1] = exp_stage(tiles[n - 1], *scored.pop(n - 1))
            if 0 <= n - 2 < len(tiles):
                value_stage(tiles[n - 2], *exped.pop(n - 2))
                if n - 2 in last:
                    finalize(*last[n - 2])

    for c in range(nq // 2):
        pl.when(i == c)(functools.partial(run_pair, c))


def _attention(qt, k, vt, szt, batch, seq, heads, tk):
    tq = 2 * tk
    nq = seq // tq
    hps = ATTN_HEADS_PER_STEP if heads % ATTN_HEADS_PER_STEP == 0 else 1
    q_spec = lambda f: pl.BlockSpec((hps * Q_HEAD_ROWS, tq), lambda b, h, i: (h, b * nq + f(i)))
    sz_spec = lambda f: pl.BlockSpec((hps * V_HEAD_DIM, tq), lambda b, h, i: (h, b * nq + f(i)))
    first = lambda i: i
    second = lambda i: nq - 1 - i
    return pl.pallas_call(
        functools.partial(_attn_kernel, tk=tk, nq=nq, heads=hps),
        grid=(batch, heads // hps, nq // 2),
        in_specs=[
            q_spec(first), q_spec(second),
            pl.BlockSpec((seq, hps * HEAD_PAD), lambda b, h, i: (b, h)),
            pl.BlockSpec((seq // tk, hps * V_HEAD_DIM, tk), lambda b, h, i: (b, h, 0)),
            sz_spec(first), sz_spec(second),
        ],
        out_specs=pl.BlockSpec((None, None, 2, hps * V_HEAD_DIM, tq),
                               lambda b, h, i: (b, i, 0, h, 0)),
        out_shape=jax.ShapeDtypeStruct((batch, nq // 2, 2, heads * V_HEAD_DIM, tq), BF16),
        scratch_shapes=[
            pltpu.VMEM((2 * hps, 1, tq), F32),
            pltpu.VMEM((2 * hps, V_HEAD_DIM + ONES_ROWS, tq), F32),
        ],
        compiler_params=_params("arbitrary", "arbitrary", "arbitrary"),
        name="mla_attention",
    )(qt, qt, k, vt, szt, szt)


def _rope_tables(seq):
    pos = jnp.arange(seq, dtype=F32)
    inv_freq = ROPE_THETA ** (-jnp.arange(0, QK_ROPE_DIM, 2, dtype=F32) / QK_ROPE_DIM)
    ang = pos[:, None] * inv_freq[None, :]
    cos, sin = jnp.cos(ang), jnp.sin(ang)
    ones = jnp.ones((seq, ROPE_LANE0), F32)
    zeros = jnp.zeros((seq, ROPE_LANE0), F32)
    return (jnp.concatenate([ones, cos, cos], axis=-1),
            jnp.concatenate([zeros, -sin, sin], axis=-1), cos.T, sin.T)


def _mla_weight_layout(w_in, gq, w_uq, w_ukv, heads):
    nb, d, _ = w_in.shape
    o1, o2, o3 = Q_LORA_RANK, Q_LORA_RANK + KV_LORA_RANK, Q_LORA_RANK + KV_LORA_RANK + QK_ROPE_DIM
    zc = lambda n: jnp.zeros((nb, d, n), w_in.dtype)
    w_lat = jnp.concatenate([
        w_in[..., :o1], zc(LATENT_PAD - Q_LORA_RANK),
        w_in[..., o1:o2], zc(LATENT_PAD - KV_LORA_RANK),
        zc(ROPE_LANE0), w_in[..., o2:o3],
    ], axis=-1).astype(BF16)
    wzt = jnp.swapaxes(w_in[..., o3:], 1, 2).astype(BF16)
    gq_l = jnp.pad(gq, ((0, 0), (0, LATENT_PAD - Q_LORA_RANK)))[:, None, :]
    wqt = jnp.swapaxes(
        jnp.pad(w_uq, ((0, 0), (0, LATENT_PAD - Q_LORA_RANK), (0, 0))), 1, 2).astype(BF16)
    wkv = w_ukv.reshape(nb, KV_LORA_RANK, heads, QK_NOPE_DIM + V_HEAD_DIM)
    wk = wkv[..., :QK_NOPE_DIM].reshape(nb, KV_LORA_RANK, heads * QK_NOPE_DIM).astype(BF16)
    wvt = jnp.swapaxes(
        wkv[..., QK_NOPE_DIM:].reshape(nb, KV_LORA_RANK, heads * V_HEAD_DIM), 1, 2).astype(BF16)
    return w_lat, wzt, gq_l, wqt, wk, wvt


def kernel(x, c, ada_w, ada_b, pre_g, post_g, sgu_w_in, sgu_norm_g, sgu_w_s, sgu_b_s, sgu_w_out,
           mla_w_in, mla_q_norm_g, mla_kv_norm_g, mla_w_uq, mla_w_ukv, mla_w_out):
    batch, seq, d = x.shape
    depth = ada_w.shape[0]
    heads = mla_w_out.shape[1] // V_HEAD_DIM
    tk = ATTN_KV_TILE
    assert seq % SGU_BLOCK == 0 and seq % (2 * tk) == 0 and d % LANES == 0
    assert sgu_w_in.shape[2] % (3 * SGU_GROUP_DIM) == 0
    assert mla_kv_norm_g.shape[1] == KV_LORA_RANK == LATENT_PAD

    mod = _adaln_mod(c, ada_w, ada_b).reshape(depth, batch, 3, 1, d)
    pre_g3 = pre_g[:, None, :]
    post_g3 = post_g[:, None, :]
    sgu_w_in_l = sgu_w_in.astype(BF16)
    sgu_w_out_l = sgu_w_out.astype(BF16)
    sgu_norm_g3 = sgu_norm_g[:, None, :]
    sgu_b_s4 = sgu_b_s[..., None]
    w_lat, wzt, gq_l, wqt, wk, wvt = _mla_weight_layout(
        mla_w_in, mla_q_norm_g, mla_w_uq, mla_w_ukv, heads)
    gkv_l = mla_kv_norm_g[:, None, :]
    mla_w_out_l = mla_w_out.astype(BF16)
    cos_t, sin_t, cos_tt, sin_tt = _rope_tables(seq)

    xf = x.reshape(batch * seq, d)
    for i in range(depth):
        j = i // 2
        if i % 2 == 0:
            h, vg, mu, rstd = _sgu_v(xf, pre_g3, mod, sgu_w_in_l, i, j, seq)
            y = _sgu_gate(h, sgu_w_in_l, vg, mu, rstd, sgu_norm_g3, sgu_w_s, sgu_b_s4, j, seq)
            w_out = sgu_w_out_l
        else:
            cq, ckv, kr, szt = _mla_in(xf, pre_g3, mod, w_lat, wzt, gq_l, gkv_l, cos_t, sin_t,
                                       i, j, seq)
            qt, k, vt = _mla_up(cq, ckv, kr, wqt, wk, wvt, cos_tt, sin_tt, j, seq, heads, tk)
            y = _attention(qt, k, vt, szt, batch, seq, heads, tk)
            w_out = mla_w_out_l
        xf = _out_proj(y, w_out, j, xf, post_g3, mod, i, seq)
    return xf.reshape(batch, seq, d)
```

```python
import functools

import jax
import jax.numpy as jnp
from jax import lax
from jax.experimental import pallas as pl
from jax.experimental.pallas import tpu as pltpu

F32 = jnp.float32
BF16 = jnp.bfloat16

NORM_EPS = 1e-6
CHUNK = 64
SGU_BLOCK = 128
SGU_GROUP_DIM = 256
Q_LORA_RANK = 448
KV_LORA_RANK = 512
QK_NOPE_DIM = 128
QK_ROPE_DIM = 64
V_HEAD_DIM = 128
ROPE_THETA = 10000.0
LANES = 128
LATENT_PAD = 512
HEAD_PAD = 2 * LANES
Q_HEAD_ROWS = QK_NOPE_DIM + QK_ROPE_DIM
ROPE_LANE0 = LANES - QK_ROPE_DIM
ATTN_KV_TILE = 256
ATTN_HEADS_PER_STEP = 2
ONES_ROWS = 16
VMEM_LIMIT_BYTES = 60 * 1024 * 1024
_NN_DIMS = (((1,), (0,)), ((), ()))
_NT_DIMS = (((1,), (1,)), ((), ()))
_TN_DIMS = (((0,), (0,)), ((), ()))


def _tile(n, pref, mult=8):
    t = min(n, pref)
    while t > mult and (n % t or t % mult):
        t -= mult
    return t if n % t == 0 else n


def _params(*semantics):
    return pltpu.CompilerParams(dimension_semantics=semantics, vmem_limit_bytes=VMEM_LIMIT_BYTES)


def _gelu(x):
    return 0.5 * x * (1.0 + lax.erf(x * (0.5 ** 0.5)))


def _silu(x):
    hx = 0.5 * x
    return hx + hx * jnp.tanh(hx)


def _modulated_norm(x, g, shift, scale):
    ms = jnp.mean(x * x, axis=-1, keepdims=True)
    y = x * lax.rsqrt(ms + NORM_EPS) * g
    return y * (1.0 + scale) + shift


def _rope_rotate(x, cos_t, sin_t):
    lane = lax.broadcasted_iota(jnp.int32, x.shape, 1)
    half = QK_ROPE_DIM // 2
    partner = jnp.where(lane < ROPE_LANE0 + half,
                        pltpu.roll(x, LANES - half, 1),
                        pltpu.roll(x, half, 1))
    return x * cos_t + partner * sin_t


ROW_BLOCK = 256
TOKEN_TILE = 512
SGU_GATE_TOKEN_TILE = 1024
MOD_COL_TILE = 1536


def _pipelined(n_blocks, produce, consume):
    pending = None
    for r in range(n_blocks):
        current = produce(r)
        if pending is not None:
            consume(r - 1, pending)
        pending = current
    consume(n_blocks - 1, pending)


def _layer_spec(layer, rows, cols):
    return pl.BlockSpec((None, rows, cols), lambda *_: (layer, 0, 0), pipeline_mode=pl.Buffered(1))


def _mod_specs(layer, tiles_per_batch, d):
    def spec(k):
        return pl.BlockSpec((None, None, None, 1, d),
                            lambda m, *_: (layer, m // tiles_per_batch, k, 0, 0))
    return spec(0), spec(1), spec(2)


def _mod_kernel(c_ref, w_ref, b_ref, o_ref):
    cond = _silu(c_ref[...]).astype(BF16)
    o_ref[0] = jnp.dot(cond, w_ref[0].astype(BF16), preferred_element_type=F32) + b_ref[0]


def _adaln_mod(c, ada_w, ada_b):
    depth, d, n = ada_w.shape
    b = c.shape[0]
    tn = _tile(n, MOD_COL_TILE, LANES)
    return pl.pallas_call(
        _mod_kernel,
        grid=(depth, n // tn),
        in_specs=[
            pl.BlockSpec((b, d), lambda i, j: (0, 0)),
            pl.BlockSpec((1, d, tn), lambda i, j: (i, 0, j)),
            pl.BlockSpec((1, 1, tn), lambda i, j: (i, 0, j)),
        ],
        out_specs=pl.BlockSpec((1, b, tn), lambda i, j: (i, 0, j)),
        out_shape=jax.ShapeDtypeStruct((depth, b, n), F32),
        compiler_params=_params("arbitrary", "arbitrary"),
        name="adaln_mod",
    )(c, ada_w, ada_b.reshape(depth, 1, n))


def _sgu_v_kernel(x_ref, g_ref, shift_ref, scale_ref, w_ref, h_ref, vg_ref, mu_ref, rstd_ref):
    tm = x_ref.shape[0]
    tr = min(ROW_BLOCK, tm)

    def project(r):
        rows = slice(r * tr, (r + 1) * tr)
        h = _modulated_norm(x_ref[rows, :], g_ref[...], shift_ref[...], scale_ref[...]).astype(BF16)
        h_ref[rows, :] = h
        return jnp.dot(h, w_ref[...], preferred_element_type=F32)

    def activate(r, acc):
        rows = slice(r * tr, (r + 1) * tr)
        v = _gelu(acc)
        vg_ref[rows, :] = v.astype(BF16)
        mu = jnp.mean(v, axis=-1, keepdims=True)
        var = jnp.mean(v * v, axis=-1, keepdims=True) - mu * mu
        mu_ref[rows, :] = mu
        rstd_ref[rows, :] = lax.rsqrt(var + NORM_EPS)

    _pipelined(tm // tr, project, activate)


def _sgu_v(x, pre_g, mod, w_in, layer, mixer_layer, seq):
    m, d = x.shape
    e = w_in.shape[2] // 3
    tm = _tile(seq, TOKEN_TILE, SGU_BLOCK)
    row = pl.BlockSpec((tm, d), lambda i: (i, 0))
    stat = pl.BlockSpec((tm, 1), lambda i: (i, 0))
    shift_spec, scale_spec, _ = _mod_specs(layer, seq // tm, d)
    return pl.pallas_call(
        _sgu_v_kernel,
        grid=(m // tm,),
        in_specs=[
            row, _layer_spec(layer, 1, d), shift_spec, scale_spec,
            pl.BlockSpec((None, d, e), lambda i: (mixer_layer, 0, 1), pipeline_mode=pl.Buffered(1)),
        ],
        out_specs=[row, pl.BlockSpec((tm, e), lambda i: (i, 0)), stat, stat],
        out_shape=[
            jax.ShapeDtypeStruct((m, d), BF16),
            jax.ShapeDtypeStruct((m, e), BF16),
            jax.ShapeDtypeStruct((m, 1), F32),
            jax.ShapeDtypeStruct((m, 1), F32),
        ],
        compiler_params=_params("arbitrary"),
        name="sgu_v",
    )(x, pre_g, mod, mod, w_in)


def _sgu_gate_kernel(h_ref, wu_ref, wz_ref, vg_ref, mu_ref, rstd_ref, ng_ref, ws_ref, bs_ref,
                     y_ref, *, groups_per_step):
    tm = h_ref.shape[0]
    tr = min(ROW_BLOCK, tm)
    nblk = tr // SGU_BLOCK
    t_chunk = lax.broadcasted_iota(jnp.int32, (SGU_BLOCK, SGU_BLOCK), 0) // CHUNK
    s_chunk = lax.broadcasted_iota(jnp.int32, (SGU_BLOCK, SGU_BLOCK), 1) // CHUNK
    causal = s_chunk <= t_chunk
    ws = [jnp.where(causal, ws_ref[gi], 0.0).astype(BF16) for gi in range(groups_per_step)]

    def project(r):
        rows = slice(r * tr, (r + 1) * tr)
        h = h_ref[rows, :]
        vn = ((vg_ref[rows, :].astype(F32) - mu_ref[rows, :]) * rstd_ref[rows, :]
              * ng_ref[...]).astype(BF16)
        vms = []
        for gi in range(groups_per_step):
            cols = slice(gi * SGU_GROUP_DIM, (gi + 1) * SGU_GROUP_DIM)
            vcat = jnp.concatenate(
                [vn[b * SGU_BLOCK:(b + 1) * SGU_BLOCK, cols] for b in range(nblk)], axis=1)
            vms.append(jnp.dot(ws[gi], vcat, preferred_element_type=F32))
        return (jnp.dot(h, wu_ref[...], preferred_element_type=F32),
                jnp.dot(h, wz_ref[...], preferred_element_type=F32), vms)

    def gate(r, projected):
        u_acc, z_acc, vms = projected
        u = _gelu(u_acc)
        z = _silu(z_acc)
        for gi in range(groups_per_step):
            cols = slice(gi * SGU_GROUP_DIM, (gi + 1) * SGU_GROUP_DIM)
            vm = vms[gi] + bs_ref[gi]
            for b in range(nblk):
                rows = slice(b * SGU_BLOCK, (b + 1) * SGU_BLOCK)
                vmb = vm[:, b * SGU_GROUP_DIM:(b + 1) * SGU_GROUP_DIM]
                y_ref[r * tr + b * SGU_BLOCK:r * tr + (b + 1) * SGU_BLOCK, cols] = (
                    u[rows, cols] * vmb * z[rows, cols]).astype(BF16)

    _pipelined(tm // tr, project, gate)


def _sgu_gate(h, w_in, vg, mu, rstd, norm_g, w_s, b_s, layer, seq):
    m, d = h.shape
    e = w_in.shape[2] // 3
    groups = e // SGU_GROUP_DIM
    gs = next(g for g in (4, 2, 1) if groups % g == 0)
    tn = gs * SGU_GROUP_DIM
    tm = _tile(seq, SGU_GATE_TOKEN_TILE, SGU_BLOCK)
    return pl.pallas_call(
        functools.partial(_sgu_gate_kernel, groups_per_step=gs),
        grid=(m // tm, groups // gs),
        in_specs=[
            pl.BlockSpec((tm, d), lambda i, j: (i, 0)),
            pl.BlockSpec((None, d, tn), lambda i, j: (layer, 0, j)),
            pl.BlockSpec((None, d, tn), lambda i, j: (layer, 0, 2 * (e // tn) + j)),
            pl.BlockSpec((tm, tn), lambda i, j: (i, j)),
            pl.BlockSpec((tm, 1), lambda i, j: (i, 0)),
            pl.BlockSpec((tm, 1), lambda i, j: (i, 0)),
            pl.BlockSpec((None, 1, tn), lambda i, j: (layer, 0, j)),
            pl.BlockSpec((None, gs, SGU_BLOCK, SGU_BLOCK), lambda i, j: (layer, j, 0, 0)),
            pl.BlockSpec((None, gs, SGU_BLOCK, 1), lambda i, j: (layer, j, 0, 0)),
        ],
        out_specs=pl.BlockSpec((tm, tn), lambda i, j: (i, j)),
        out_shape=jax.ShapeDtypeStruct((m, e), BF16),
        compiler_params=_params("arbitrary", "arbitrary"),
        name="sgu_gate",
    )(h, w_in, w_in, vg, mu, rstd, norm_g, w_s, b_s)


def _out_proj_kernel(y_ref, w_ref, x_ref, g_ref, gate_ref, o_ref, *, y_transposed):
    tm = o_ref.shape[0]
    tr = min(ROW_BLOCK, tm)

    def project(r):
        if y_transposed:
            return lax.dot_general(y_ref[:, r * tr:(r + 1) * tr], w_ref[...], _TN_DIMS,
                                   preferred_element_type=F32)
        return jnp.dot(y_ref[r * tr:(r + 1) * tr, :], w_ref[...], preferred_element_type=F32)

    def residual(r, acc):
        rows = slice(r * tr, (r + 1) * tr)
        ms = jnp.mean(acc * acc, axis=-1, keepdims=True)
        yn = acc * lax.rsqrt(ms + NORM_EPS) * g_ref[...]
        o_ref[rows, :] = x_ref[rows, :] + gate_ref[...] * yn

    _pipelined(tm // tr, project, residual)


def _out_proj(y, w_out, mixer_layer, x, post_g, mod, layer, seq):
    m, d = x.shape
    k = w_out.shape[1]
    if y.ndim == 2:
        tm = _tile(seq, TOKEN_TILE)
        y_spec = pl.BlockSpec((tm, k), lambda i: (i, 0))
    else:
        tm = y.shape[4]
        nq = seq // tm

        def pair_major(i):
            j = i % nq
            lower = j < nq // 2
            return (i // nq, jnp.where(lower, j, nq - 1 - j), jnp.where(lower, 0, 1), 0, 0)

        y_spec = pl.BlockSpec((None, None, None, k, tm), pair_major)
    row = pl.BlockSpec((tm, d), lambda i: (i, 0))
    _, _, gate_spec = _mod_specs(layer, seq // tm, d)
    return pl.pallas_call(
        functools.partial(_out_proj_kernel, y_transposed=y.ndim != 2),
        grid=(m // tm,),
        in_specs=[y_spec, _layer_spec(mixer_layer, k, d), row, _layer_spec(layer, 1, d), gate_spec],
        out_specs=row,
        out_shape=jax.ShapeDtypeStruct((m, d), F32),
        compiler_params=_params("arbitrary"),
        name="out_proj",
    )(y, w_out, x, post_g, mod)


def _mla_in_kernel(x_ref, g_ref, shift_ref, scale_ref, w_ref, wzt_ref, gq_ref, gkv_ref, cos_ref,
                   sin_ref, cq_ref, ckv_ref, kr_ref, szt_ref):
    h = _modulated_norm(x_ref[...], g_ref[...], shift_ref[...], scale_ref[...]).astype(BF16)
    lat = jnp.dot(h, w_ref[...], preferred_element_type=F32)
    cq = lat[:, :LATENT_PAD]
    ckv = lat[:, LATENT_PAD:2 * LATENT_PAD]
    kr = lat[:, 2 * LATENT_PAD:]
    ms_q = jnp.sum(cq * cq, axis=-1, keepdims=True) * (1.0 / Q_LORA_RANK)
    cq_ref[...] = (cq * lax.rsqrt(ms_q + NORM_EPS) * gq_ref[...]).astype(BF16)
    ms_kv = jnp.mean(ckv * ckv, axis=-1, keepdims=True)
    ckv_ref[...] = (ckv * lax.rsqrt(ms_kv + NORM_EPS) * gkv_ref[...]).astype(BF16)
    kr_ref[...] = _rope_rotate(kr, cos_ref[...], sin_ref[...]).astype(BF16)
    zt = lax.dot_general(wzt_ref[...], h, _NT_DIMS, preferred_element_type=F32)
    szt_ref[...] = _silu(zt).astype(BF16)


def _mla_in(x, pre_g, mod, w_lat, wzt, gq, gkv, cos_t, sin_t, layer, mixer_layer, seq):
    m, d = x.shape
    nlat = w_lat.shape[2]
    width = wzt.shape[1]
    tm = _tile(seq, TOKEN_TILE)
    tpb = seq // tm
    row = lambda n: pl.BlockSpec((tm, n), lambda i: (i, 0))
    table = pl.BlockSpec((tm, LANES), lambda i: (i % tpb, 0))
    shift_spec, scale_spec, _ = _mod_specs(layer, tpb, d)
    j = mixer_layer
    return pl.pallas_call(
        _mla_in_kernel,
        grid=(m // tm,),
        in_specs=[
            row(d), _layer_spec(layer, 1, d), shift_spec, scale_spec,
            _layer_spec(j, d, nlat), _layer_spec(j, width, d),
            _layer_spec(j, 1, LATENT_PAD), _layer_spec(j, 1, LATENT_PAD), table, table,
        ],
        out_specs=[row(LATENT_PAD), row(LATENT_PAD), row(LANES),
                   pl.BlockSpec((width, tm), lambda i: (0, i))],
        out_shape=[
            jax.ShapeDtypeStruct((m, LATENT_PAD), BF16),
            jax.ShapeDtypeStruct((m, LATENT_PAD), BF16),
            jax.ShapeDtypeStruct((m, LANES), BF16),
            jax.ShapeDtypeStruct((width, m), BF16),
        ],
        compiler_params=_params("arbitrary"),
        name="mla_in",
    )(x, pre_g, mod, mod, w_lat, wzt, gq, gkv, cos_t, sin_t)


def _mla_up_kernel(cq_ref, ckv_ref, kr_ref, wqt_ref, wk_ref, wvt_ref, cos_ref, sin_ref,
                   qt_ref, k_ref, vt_ref, *, heads, scale, tk):
    ckv = ckv_ref[...]
    qt = lax.dot_general(wqt_ref[...], cq_ref[...], _NT_DIMS, preferred_element_type=F32) * scale
    cos_t = cos_ref[...]
    sin_t = sin_ref[...]
    half = QK_ROPE_DIM // 2
    for hd in range(heads):
        r0 = hd * Q_HEAD_ROWS
        r1 = r0 + QK_NOPE_DIM
        qt_ref[r0:r1, :] = qt[r0:r1].astype(BF16)
        x1 = qt[r1:r1 + half]
        x2 = qt[r1 + half:r1 + QK_ROPE_DIM]
        qt_ref[r1:r1 + half, :] = (x1 * cos_t - x2 * sin_t).astype(BF16)
        qt_ref[r1 + half:r1 + QK_ROPE_DIM, :] = (x1 * sin_t + x2 * cos_t).astype(BF16)
    kn = jnp.dot(ckv, wk_ref[...], preferred_element_type=F32)
    kr = kr_ref[...]
    for hd in range(heads):
        c0 = hd * HEAD_PAD
        k_ref[:, c0:c0 + LANES] = kn[:, hd * QK_NOPE_DIM:(hd + 1) * QK_NOPE_DIM].astype(BF16)
        k_ref[:, c0 + LANES:c0 + HEAD_PAD] = kr
    vt = lax.dot_general(wvt_ref[...], ckv, _NT_DIMS, preferred_element_type=F32)
    for cb in range(vt.shape[1] // tk):
        vt_ref[cb] = vt[:, cb * tk:(cb + 1) * tk].astype(BF16)


def _mla_up(cq, ckv, kr, wqt, wk, wvt, cos_tt, sin_tt, layer, seq, heads, tk):
    m = cq.shape[0]
    tm = _tile(seq, TOKEN_TILE, tk)
    tpb = seq // tm
    row = lambda n: pl.BlockSpec((tm, n), lambda i: (i, 0))
    whole = lambda a: _layer_spec(layer, a.shape[1], a.shape[2])
    table = pl.BlockSpec((QK_ROPE_DIM // 2, tm), lambda i: (0, i % tpb))
    scale = float((QK_NOPE_DIM + QK_ROPE_DIM) ** -0.5 * 1.4426950408889634)
    return pl.pallas_call(
        functools.partial(_mla_up_kernel, heads=heads, scale=scale, tk=tk),
        grid=(m // tm,),
        in_specs=[row(LATENT_PAD), row(LATENT_PAD), row(LANES), whole(wqt), whole(wk), whole(wvt),
                  table, table],
        out_specs=[
            pl.BlockSpec((heads * Q_HEAD_ROWS, tm), lambda i: (0, i)),
            row(heads * HEAD_PAD),
            pl.BlockSpec((tm // tk, heads * V_HEAD_DIM, tk), lambda i: (i, 0, 0)),
        ],
        out_shape=[
            jax.ShapeDtypeStruct((heads * Q_HEAD_ROWS, m), BF16),
            jax.ShapeDtypeStruct((m, heads * HEAD_PAD), BF16),
            jax.ShapeDtypeStruct((m // tk, heads * V_HEAD_DIM, tk), BF16),
        ],
        compiler_params=_params("arbitrary"),
        name="mla_up",
    )(cq, ckv, kr, wqt, wk, wvt, cos_tt, sin_tt)


def _attn_kernel(qa_ref, qb_ref, k_ref, vt_ref, sza_ref, szb_ref, o_ref, m_sc, acc_sc,
                 *, tk, nq, heads):
    i = pl.program_id(2)
    tq = 2 * tk
    q_refs = (qa_ref, qb_ref)
    sz_refs = (sza_ref, szb_ref)

    def masked(s):
        visible = (lax.broadcasted_iota(jnp.int32, (tk, tk), 0) // CHUNK
                   <= lax.broadcasted_iota(jnp.int32, (tk, tk), 1) // CHUNK)
        return jnp.where(visible, s, -1e30)

    def score_stage(tile):
        hd, slot, t, nt, lane0, mask = tile
        k = k_ref[t * tk:(t + nt) * tk, hd * HEAD_PAD:(hd + 1) * HEAD_PAD]
        q = q_refs[slot][hd * Q_HEAD_ROWS:(hd + 1) * Q_HEAD_ROWS, lane0:]
        qt = jnp.concatenate([q[:QK_NOPE_DIM], jnp.zeros((ROPE_LANE0, q.shape[1]), BF16),
                              q[QK_NOPE_DIM:]], axis=0)
        s = jnp.dot(k, qt, preferred_element_type=F32)
        if mask == "left":
            s = jnp.concatenate([masked(s[:, :tk]), s[:, tk:]], axis=1)
        elif mask == "all":
            s = masked(s)
        return s, jnp.max(s, axis=0, keepdims=True)

    def exp_stage(tile, s, s_max):
        hd, slot, _, _, lane0, mask = tile
        state = 2 * hd + slot
        if mask == "left":
            m_sc[state] = s_max
            return jnp.exp2(s - s_max).astype(BF16), None
        lanes = pl.ds(lane0, tq - lane0)
        m_prev = m_sc[state, :, lanes]
        m_new = jnp.maximum(m_prev, s_max)
        m_sc[state, :, lanes] = m_new
        return jnp.exp2(s - m_new).astype(BF16), jnp.exp2(m_prev - m_new)

    def value_stage(tile, p, alpha):
        hd, slot, t, nt, lane0, _ = tile
        state = 2 * hd + slot
        rows = slice(hd * V_HEAD_DIM, (hd + 1) * V_HEAD_DIM)
        v_t = jnp.concatenate([vt_ref[t + u, rows, :] for u in range(nt)], axis=1)
        v_aug = jnp.concatenate([v_t, jnp.ones((ONES_ROWS, nt * tk), BF16)], axis=0)
        pv = jnp.dot(v_aug, p, preferred_element_type=F32)
        if alpha is None:
            acc_sc[state] = pv
        else:
            lanes = pl.ds(lane0, tq - lane0)
            acc_sc[state, :, lanes] = alpha * acc_sc[state, :, lanes] + pv

    def finalize(hd, slot):
        acc = acc_sc[2 * hd + slot]
        rows = slice(hd * V_HEAD_DIM, (hd + 1) * V_HEAD_DIM)
        out_t = acc[:V_HEAD_DIM] * (1.0 / acc[V_HEAD_DIM:V_HEAD_DIM + 1])
        o_ref[slot, rows, :] = (out_t * sz_refs[slot][rows, :].astype(F32)).astype(BF16)

    def run_pair(c):
        chains = []
        for hd in range(heads):
            for slot, blk in ((1, nq - 1 - c), (0, c)):
                chain = [(hd, slot, 2 * blk, 1, 0, "left"), (hd, slot, 2 * blk + 1, 1, tk, "all")]
                chain.extend((hd, slot, t, 1, 0, None) for t in range(2 * blk))
                chains.append(chain)
        tiles = [chain[n] for n in range(max(map(len, chains))) for chain in chains if n < len(chain)]
        last_of = {(t[0], t[1]): pos for pos, t in enumerate(tiles)}
        last = {pos: key for key, pos in last_of.items()}
        scored = {}
        exped = {}
        for n in range(len(tiles) + 2):
            if n < len(tiles):
                scored[n] = score_stage(tiles[n])
            if 0 <= n - 1 < len(tiles):
                exped[n - 1] = exp_stage(tiles[n - 1], *scored.pop(n - 1))
            if 0 <= n - 2 < len(tiles):
                value_stage(tiles[n - 2], *exped.pop(n - 2))
                if n - 2 in last:
                    finalize(*last[n - 2])

    for c in range(nq // 2):
        pl.when(i == c)(functools.partial(run_pair, c))


def _attention(qt, k, vt, szt, batch, seq, heads, tk):
    tq = 2 * tk
    nq = seq // tq
    hps = ATTN_HEADS_PER_STEP if heads % ATTN_HEADS_PER_STEP == 0 else 1
    q_spec = lambda f: pl.BlockSpec((hps * Q_HEAD_ROWS, tq), lambda b, h, i: (h, b * nq + f(i)))
    sz_spec = lambda f: pl.BlockSpec((hps * V_HEAD_DIM, tq), lambda b, h, i: (h, b * nq + f(i)))
    first = lambda i: i
    second = lambda i: nq - 1 - i
    return pl.pallas_call(
        functools.partial(_attn_kernel, tk=tk, nq=nq, heads=hps),
        grid=(batch, heads // hps, nq // 2),
        in_specs=[
            q_spec(first), q_spec(second),
            pl.BlockSpec((seq, hps * HEAD_PAD), lambda b, h, i: (b, h)),
            pl.BlockSpec((seq // tk, hps * V_HEAD_DIM, tk), lambda b, h, i: (b, h, 0)),
            sz_spec(first), sz_spec(second),
        ],
        out_specs=pl.BlockSpec((None, None, 2, hps * V_HEAD_DIM, tq),
                               lambda b, h, i: (b, i, 0, h, 0)),
        out_shape=jax.ShapeDtypeStruct((batch, nq // 2, 2, heads * V_HEAD_DIM, tq), BF16),
        scratch_shapes=[
            pltpu.VMEM((2 * hps, 1, tq), F32),
            pltpu.VMEM((2 * hps, V_HEAD_DIM + ONES_ROWS, tq), F32),
        ],
        compiler_params=_params("arbitrary", "arbitrary", "arbitrary"),
        name="mla_attention",
    )(qt, qt, k, vt, szt, szt)


def _rope_tables(seq):
    pos = jnp.arange(seq, dtype=F32)
    inv_freq = ROPE_THETA ** (-jnp.arange(0, QK_ROPE_DIM, 2, dtype=F32) / QK_ROPE_DIM)
    ang = pos[:, None] * inv_freq[None, :]
    cos, sin = jnp.cos(ang), jnp.sin(ang)
    ones = jnp.ones((seq, ROPE_LANE0), F32)
    zeros = jnp.zeros((seq, ROPE_LANE0), F32)
    return (jnp.concatenate([ones, cos, cos], axis=-1),
            jnp.concatenate([zeros, -sin, sin], axis=-1), cos.T, sin.T)


def _mla_weight_layout(w_in, gq, w_uq, w_ukv, heads):
    nb, d, _ = w_in.shape
    o1, o2, o3 = Q_LORA_RANK, Q_LORA_RANK + KV_LORA_RANK, Q_LORA_RANK + KV_LORA_RANK + QK_ROPE_DIM
    zc = lambda n: jnp.zeros((nb, d, n), w_in.dtype)
    w_lat = jnp.concatenate([
        w_in[..., :o1], zc(LATENT_PAD - Q_LORA_RANK),
        w_in[..., o1:o2], zc(LATENT_PAD - KV_LORA_RANK),
        zc(ROPE_LANE0), w_in[..., o2:o3],
    ], axis=-1).astype(BF16)
    wzt = jnp.swapaxes(w_in[..., o3:], 1, 2).astype(BF16)
    gq_l = jnp.pad(gq, ((0, 0), (0, LATENT_PAD - Q_LORA_RANK)))[:, None, :]
    wqt = jnp.swapaxes(
        jnp.pad(w_uq, ((0, 0), (0, LATENT_PAD - Q_LORA_RANK), (0, 0))), 1, 2).astype(BF16)
    wkv = w_ukv.reshape(nb, KV_LORA_RANK, heads, QK_NOPE_DIM + V_HEAD_DIM)
    wk = wkv[..., :QK_NOPE_DIM].reshape(nb, KV_LORA_RANK, heads * QK_NOPE_DIM).astype(BF16)
    wvt = jnp.swapaxes(
        wkv[..., QK_NOPE_DIM:].reshape(nb, KV_LORA_RANK, heads * V_HEAD_DIM), 1, 2).astype(BF16)
    return w_lat, wzt, gq_l, wqt, wk, wvt


def kernel(x, c, ada_w, ada_b, pre_g, post_g, sgu_w_in, sgu_norm_g, sgu_w_s, sgu_b_s, sgu_w_out,
           mla_w_in, mla_q_norm_g, mla_kv_norm_g, mla_w_uq, mla_w_ukv, mla_w_out):
    batch, seq, d = x.shape
    depth = ada_w.shape[0]
    heads = mla_w_out.shape[1] // V_HEAD_DIM
    tk = ATTN_KV_TILE
    assert seq % SGU_BLOCK == 0 and seq % (2 * tk) == 0 and d % LANES == 0
    assert sgu_w_in.shape[2] % (3 * SGU_GROUP_DIM) == 0
    assert mla_kv_norm_g.shape[1] == KV_LORA_RANK == LATENT_PAD

    mod = _adaln_mod(c, ada_w, ada_b).reshape(depth, batch, 3, 1, d)
    pre_g3 = pre_g[:, None, :]
    post_g3 = post_g[:, None, :]
    sgu_w_in_l = sgu_w_in.astype(BF16)
    sgu_w_out_l = sgu_w_out.astype(BF16)
    sgu_norm_g3 = sgu_norm_g[:, None, :]
    sgu_b_s4 = sgu_b_s[..., None]
    w_lat, wzt, gq_l, wqt, wk, wvt = _mla_weight_layout(
        mla_w_in, mla_q_norm_g, mla_w_uq, mla_w_ukv, heads)
    gkv_l = mla_kv_norm_g[:, None, :]
    mla_w_out_l = mla_w_out.astype(BF16)
    cos_t, sin_t, cos_tt, sin_tt = _rope_tables(seq)

    xf = x.reshape(batch * seq, d)
    for i in range(depth):
        j = i // 2
        if i % 2 == 0:
            h, vg, mu, rstd = _sgu_v(xf, pre_g3, mod, sgu_w_in_l, i, j, seq)
            y = _sgu_gate(h, sgu_w_in_l, vg, mu, rstd, sgu_norm_g3, sgu_w_s, sgu_b_s4, j, seq)
            w_out = sgu_w_out_l
        else:
            cq, ckv, kr, szt = _mla_in(xf, pre_g3, mod, w_lat, wzt, gq_l, gkv_l, cos_t, sin_t,
                                       i, j, seq)
            qt, k, vt = _mla_up(cq, ckv, kr, wqt, wk, wvt, cos_tt, sin_tt, j, seq, heads, tk)
            y = _attention(qt, k, vt, szt, batch, seq, heads, tk)
            w_out = mla_w_out_l
        xf = _out_proj(y, w_out, j, xf, post_g3, mod, i, seq)
    return xf.reshape(batch, seq, d)
```
